```python
import jax, jax.numpy as jnp
from jax import lax
import numpy as np

D_MODEL = 1024
BATCH = 8
SEQ = 2048
DEPTH = 1
DEC_BATCH = 128
DEC_SEQ = 4
PAST_LEN = 8192
PAGE_SIZE = 128

MIX_DIM = D_MODEL
A_HEADS = 8
A_HEAD_DIM = (MIX_DIM // 2) // A_HEADS
MOBA_BLOCK = 256
MOBA_TOPK = 3
MOBA_QBLOCK = 128
B_HEADS = 4
B_HEAD_DIM = (MIX_DIM // 2) // B_HEADS
CONV_W = 4
GDN_CHUNK = 64
D_FF = ((8 * D_MODEL // 3 + 255) // 256) * 256
RMS_EPS = 1e-6
L2_EPS = 1e-6
NEG_INF = -1e30

A_WIDTH = A_HEADS * A_HEAD_DIM
B_WIDTH = B_HEADS * B_HEAD_DIM
CONV_DIM = 3 * B_WIDTH
IN_DIM = 3 * A_WIDTH + CONV_DIM + B_WIDTH + 2 * B_HEADS
IN_SPLITS = [A_WIDTH, 2 * A_WIDTH, 3 * A_WIDTH, 3 * A_WIDTH + CONV_DIM,
             3 * A_WIDTH + CONV_DIM + B_WIDTH, 3 * A_WIDTH + CONV_DIM + B_WIDTH + B_HEADS]

kernel_name = 'hybrid_moba_gdn_macaron_step'


def _rms(x, w):
    xf = x.astype(jnp.float32)
    y = xf * lax.rsqrt(jnp.mean(xf * xf, axis=-1, keepdims=True) + RMS_EPS)
    return (y * w.astype(jnp.float32)).astype(x.dtype)


def _l2norm(x):
    return x * lax.rsqrt(jnp.sum(x * x, axis=-1, keepdims=True) + L2_EPS)


def _swiglu(x, w_in, w_out):
    gate, up = jnp.split(x @ w_in, 2, axis=-1)
    return (jax.nn.silu(gate) * up) @ w_out


def _causal_conv(x, buf, w):
    t = x.shape[1]
    xp = jnp.concatenate([buf.astype(x.dtype), x], axis=1)
    y = xp[:, 0:t] * w[0]
    for j in range(1, CONV_W):
        y = y + xp[:, j:j + t] * w[j]
    return jax.nn.silu(y), xp[:, t:]


def _block_means(k):
    return k.astype(jnp.float32).reshape(-1, MOBA_BLOCK, k.shape[-2], k.shape[-1]).mean(axis=1)


def _moba_core(q, q_pos, k_mean, gather_kv):
    t, h, d = q.shape
    nb = k_mean.shape[0]
    qf = q.astype(jnp.float32)
    own = q_pos // MOBA_BLOCK
    s = jnp.einsum('thd,nhd->thn', qf, k_mean)
    past = jnp.broadcast_to((jnp.arange(nb)[None, :] < own[:, None])[:, None, :], (t, h, nb))
    s = jnp.where(past, s, NEG_INF)
    if nb < MOBA_TOPK:
        extra = MOBA_TOPK - nb
        s = jnp.pad(s, ((0, 0), (0, 0), (0, extra)), constant_values=NEG_INF)
        past = jnp.pad(past, ((0, 0), (0, 0), (0, extra)), constant_values=False)
    _, top_i = lax.top_k(s, MOBA_TOPK)
    top_ok = jnp.take_along_axis(past, top_i, axis=-1)
    top_i = jnp.minimum(top_i, nb - 1)
    blk = jnp.concatenate([top_i, jnp.broadcast_to(own[:, None, None], (t, h, 1))], axis=-1)
    ok = jnp.concatenate([top_ok, jnp.ones((t, h, 1), dtype=bool)], axis=-1)
    kpos = blk[..., None] * MOBA_BLOCK + jnp.arange(MOBA_BLOCK)
    k_sel, v_sel = gather_kv(kpos)
    mask = ok[..., None] & (kpos <= q_pos[:, None, None, None])
    logits = jnp.einsum('thd,thnkd->thnk', qf, k_sel.astype(jnp.float32)) * (d ** -0.5)
    logits = jnp.where(mask, logits, NEG_INF)
    p = jax.nn.softmax(logits.reshape(t, h, -1), axis=-1).reshape(logits.shape)
    out = jnp.einsum('thnk,thnkd->thd', p, v_sel.astype(jnp.float32))
    return out.astype(q.dtype)


def _moba_prompt(q, k, v):
    b, s, h, d = q.shape
    s_pad = -(-s // MOBA_BLOCK) * MOBA_BLOCK
    qb = min(MOBA_QBLOCK, s)
    n_qb = s // qb
    pos = jnp.arange(s, dtype=jnp.int32).reshape(n_qb, qb)
    hidx = jnp.arange(h)[None, :, None, None]

    def per_seq(args):
        qs, ks, vs = args
        kp = jnp.pad(ks, ((0, s_pad - s), (0, 0), (0, 0)))
        vp = jnp.pad(vs, ((0, s_pad - s), (0, 0), (0, 0)))
        k_mean = _block_means(kp)
        gather = lambda kpos: (kp[kpos, hidx], vp[kpos, hidx])
        out = lax.map(lambda a: _moba_core(a[0], a[1], k_mean, gather),
                      (qs.reshape(n_qb, qb, h, d), pos))
        return out.reshape(s, h, d)

    return lax.map(per_seq, (q, k, v))


def _moba_sample(q, k, v, pool_k, pool_v, page_table):
    b, t, h, d = q.shape
    past = page_table.shape[1] * PAGE_SIZE
    total = past + t
    l_pad = -(-total // MOBA_BLOCK) * MOBA_BLOCK
    q_pos = past + jnp.arange(t, dtype=jnp.int32)
    hidx = jnp.arange(h)[None, :, None, None]

    def per_seq(args):
        qs, ks, vs, pt = args
        k_past = pool_k[pt].reshape(past, h, d)
        k_all = jnp.pad(jnp.concatenate([k_past, ks.astype(k_past.dtype)], axis=0),
                        ((0, l_pad - total), (0, 0), (0, 0)))
        k_mean = _block_means(k_all)

        def gather(kpos):
            in_past = kpos < past
            pp = jnp.minimum(kpos, past - 1)
            v_past = pool_v[pt[pp // PAGE_SIZE], pp % PAGE_SIZE, hidx]
            v_new = vs[jnp.clip(kpos - past, 0, t - 1), hidx]
            v_sel = jnp.where(in_past[..., None], v_past.astype(vs.dtype), v_new)
            return k_all[kpos, hidx], v_sel

        return _moba_core(qs, q_pos, k_mean, gather)

    return lax.map(per_seq, (q, k, v, page_table))


def _gated_delta(q, k, v, g, beta, s0):
    bsz, t, h, _ = q.shape
    dv = v.shape[-1]
    c = min(GDN_CHUNK, t)
    n = -(-t // c)
    pad = n * c - t

    def to_chunks(a):
        a = jnp.pad(a, [(0, 0), (0, pad)] + [(0, 0)] * (a.ndim - 2))
        a = a.reshape((bsz, n, c) + a.shape[2:])
        return jnp.moveaxis(a, (1, 3), (0, 2))

    q, k, v, g, beta = (to_chunks(a) for a in (q, k, v, g, beta))
    gc = jnp.cumsum(g, axis=-1)
    idx = jnp.arange(c)
    causal = idx[:, None] >= idx[None, :]
    strict = idx[:, None] > idx[None, :]
    diff = gc[..., :, None] - gc[..., None, :]
    decay = jnp.where(causal, jnp.exp(jnp.where(causal, diff, 0.0)), 0.0)
    kb = k * beta[..., None]
    lmat = jnp.where(strict, jnp.einsum('nbhid,nbhjd->nbhij', kb, k) * decay, 0.0)
    eye = jnp.eye(c, dtype=jnp.float32)
    tinv = lax.linalg.triangular_solve(lmat + eye, jnp.broadcast_to(eye, lmat.shape),
                                       left_side=True, lower=True, unit_diagonal=True)
    u = jnp.einsum('nbhij,nbhjd->nbhid', tinv, v * beta[..., None])
    w = jnp.einsum('nbhij,nbhjd->nbhid', tinv, kb * jnp.exp(gc)[..., None])

    def step(s, xs):
        qi, ki, ui, wi, gi, di = xs
        v_new = ui - jnp.einsum('bhcd,bhde->bhce', wi, s)
        attn = jnp.einsum('bhid,bhjd->bhij', qi, ki) * di
        o = (jnp.einsum('bhcd,bhde->bhce', qi * jnp.exp(gi)[..., None], s)
             + jnp.einsum('bhij,bhje->bhie', attn, v_new))
        gl = gi[..., -1]
        s = (s * jnp.exp(gl)[..., None, None]
             + jnp.einsum('bhcd,bhce->bhde', ki * jnp.exp(gl[..., None] - gi)[..., None], v_new))
        return s, o

    s_fin, o = lax.scan(step, s0, (q, k, u, w, gc, decay))
    o = jnp.moveaxis(o, (0, 2), (1, 3)).reshape(bsz, n * c, h, dv)[:, :t]
    return o, s_fin


def _layer(x, attn_fn, conv_buf, gdn_s0, p):
    bsz, t, _ = x.shape
    h = x + 0.5 * _rms(_swiglu(_rms(x, p['n1a']), p['w1i'], p['w1o']), p['n1b'])
    a = _rms(h, p['nma'])
    qa, ka, va, qkv, z, braw, araw = jnp.split(a @ p['wmi'], IN_SPLITS, axis=-1)
    heads_a = lambda y: y.reshape(bsz, t, A_HEADS, A_HEAD_DIM)
    qa, ka, va = heads_a(qa), heads_a(ka), heads_a(va)
    out_a = attn_fn(qa, ka, va).reshape(bsz, t, A_WIDTH)
    qkv, conv_new = _causal_conv(qkv, conv_buf, p['conv'])
    heads_b = lambda y: y.astype(jnp.float32).reshape(bsz, t, B_HEADS, B_HEAD_DIM)
    gq, gk, gv = (heads_b(y) for y in jnp.split(qkv, 3, axis=-1))
    gq = _l2norm(gq) * (B_HEAD_DIM ** -0.5)
    gk = _l2norm(gk)
    beta = jax.nn.sigmoid(braw.astype(jnp.float32))
    g = -jnp.exp(p['a_log'].astype(jnp.float32)) * jax.nn.softplus(
        araw.astype(jnp.float32) + p['dt_bias'].astype(jnp.float32))
    o, s_new = _gated_delta(gq, gk, gv, g, beta, gdn_s0.astype(jnp.float32))
    o = _rms(o, p['gnorm']) * jax.nn.silu(heads_b(z))
    out_b = o.reshape(bsz, t, B_WIDTH).astype(x.dtype)
    mix = jnp.concatenate([out_a, out_b], axis=-1) @ p['wmo']
    h = h + _rms(mix, p['nmb'])
    y = h + 0.5 * _rms(_swiglu(_rms(h, p['n2a']), p['w2i'], p['w2o']), p['n2b'])
    return y, ka, va, s_new.astype(x.dtype), conv_new


def setup_inputs(seed: int = 0) -> dict:
    key = jax.random.key(seed)
    ks = jax.random.split(key, 24)
    f32 = jnp.float32
    n_pages = PAST_LEN // PAGE_SIZE
    n_phys = (DEC_BATCH * n_pages * 5) // 4
    nrm = lambda k, shape, scale: jax.random.normal(k, shape, f32) * scale
    gain = lambda k, n: 1.0 + 0.05 * jax.random.normal(k, (DEPTH, n), f32)
    perm = jax.random.permutation(ks[6], n_phys)
    page_table = perm[:DEC_BATCH * n_pages].reshape(DEC_BATCH, n_pages).astype(jnp.int32)
    dt = jnp.exp(jax.random.uniform(ks[7], (DEPTH, B_HEADS), f32, float(np.log(1e-3)), float(np.log(1e-1))))
    return {
        'x_prompt': nrm(ks[0], (BATCH, SEQ, D_MODEL), 1.0),
        'x_sample': nrm(ks[1], (DEC_BATCH, DEC_SEQ, D_MODEL), 1.0),
        'cache_k': nrm(ks[2], (DEPTH, n_phys, PAGE_SIZE, A_HEADS, A_HEAD_DIM), 1.0),
        'cache_v': nrm(ks[3], (DEPTH, n_phys, PAGE_SIZE, A_HEADS, A_HEAD_DIM), 1.0),
        'state_gdn': nrm(ks[4], (DEPTH, DEC_BATCH, B_HEADS, B_HEAD_DIM, B_HEAD_DIM), 0.05),
        'state_conv': nrm(ks[5], (DEPTH, DEC_BATCH, CONV_W - 1, CONV_DIM), 1.0),
        'page_table': page_table,
        'norm_ffn1_pre': gain(ks[8], D_MODEL),
        'norm_ffn1_post': gain(ks[9], D_MODEL),
        'w_ffn1_in': nrm(ks[10], (DEPTH, D_MODEL, 2 * D_FF), D_MODEL ** -0.5),
        'w_ffn1_out': nrm(ks[11], (DEPTH, D_FF, D_MODEL), D_FF ** -0.5),
        'norm_mix_pre': gain(ks[12], D_MODEL),
        'norm_mix_post': gain(ks[13], D_MODEL),
        'w_mix_in': nrm(ks[14], (DEPTH, D_MODEL, IN_DIM), D_MODEL ** -0.5),
        'w_mix_out': nrm(ks[15], (DEPTH, MIX_DIM, D_MODEL), MIX_DIM ** -0.5),
        'gdn_conv_w': nrm(ks[16], (DEPTH, CONV_W, CONV_DIM), CONV_W ** -0.5),
        'gdn_a_log': jnp.log(jax.random.uniform(ks[17], (DEPTH, B_HEADS), f32, 1.0, 16.0)),
        'gdn_dt_bias': jnp.log(jnp.expm1(dt)),
        'gdn_norm_w': gain(ks[18], B_HEAD_DIM),
        'norm_ffn2_pre': gain(ks[19], D_MODEL),
        'norm_ffn2_post': gain(ks[20], D_MODEL),
        'w_ffn2_in': nrm(ks[21], (DEPTH, D_MODEL, 2 * D_FF), D_MODEL ** -0.5),
        'w_ffn2_out': nrm(ks[22], (DEPTH, D_FF, D_MODEL), D_FF ** -0.5),
    }


def reference(x_prompt, x_sample, cache_k, cache_v, state_gdn, state_conv, page_table,
              norm_ffn1_pre, norm_ffn1_post, w_ffn1_in, w_ffn1_out,
              norm_mix_pre, norm_mix_post, w_mix_in, w_mix_out,
              gdn_conv_w, gdn_a_log, gdn_dt_bias, gdn_norm_w,
              norm_ffn2_pre, norm_ffn2_post, w_ffn2_in, w_ffn2_out):
    bp = x_prompt.shape[0]
    yp, ys = x_prompt, x_sample
    kp_l, vp_l, ks_l, vs_l, sp_l, ss_l, cp_l, cs_l = [], [], [], [], [], [], [], []
    for l in range(DEPTH):
        p = {'n1a': norm_ffn1_pre[l], 'n1b': norm_ffn1_post[l], 'w1i': w_ffn1_in[l], 'w1o': w_ffn1_out[l],
             'nma': norm_mix_pre[l], 'nmb': norm_mix_post[l], 'wmi': w_mix_in[l], 'wmo': w_mix_out[l],
             'conv': gdn_conv_w[l], 'a_log': gdn_a_log[l], 'dt_bias': gdn_dt_bias[l], 'gnorm': gdn_norm_w[l],
             'n2a': norm_ffn2_pre[l], 'n2b': norm_ffn2_post[l], 'w2i': w_ffn2_in[l], 'w2o': w_ffn2_out[l]}
        pool_k, pool_v = cache_k[l], cache_v[l]
        sample_attn = lambda q, k, v: _moba_sample(q, k, v, pool_k, pool_v, page_table)
        conv0 = jnp.zeros((bp, CONV_W - 1, CONV_DIM), x_prompt.dtype)
        s0 = jnp.zeros((bp, B_HEADS, B_HEAD_DIM, B_HEAD_DIM), jnp.float32)
        yp, kp, vp, sp, cp = _layer(yp, _moba_prompt, conv0, s0, p)
        ys, kn, vn, sn, cn = _layer(ys, sample_attn, state_conv[l], state_gdn[l], p)
        kp_l.append(kp); vp_l.append(vp); ks_l.append(kn); vs_l.append(vn)
        sp_l.append(sp); ss_l.append(sn); cp_l.append(cp); cs_l.append(cn)
    k_prompt, v_prompt = jnp.stack(kp_l), jnp.stack(vp_l)
    k_sample, v_sample = jnp.stack(ks_l), jnp.stack(vs_l)
    gdn_prompt, gdn_sample = jnp.stack(sp_l), jnp.stack(ss_l)
    conv_prompt, conv_sample = jnp.stack(cp_l), jnp.stack(cs_l)
    return (yp, ys, k_prompt, v_prompt, k_sample, v_sample, gdn_prompt, gdn_sample, conv_prompt, conv_sample)
```

```python
import functools

import jax
import jax.numpy as jnp
from jax import lax
from jax.experimental import pallas as pl
from jax.experimental.pallas import tpu as pltpu

F32 = jnp.float32
BF16 = jnp.bfloat16

MOBA_BLOCK = 256
MOBA_TOPK = 3
GDN_CHUNK = 64
RMS_EPS = 1e-6
L2_EPS = 1e-6
NEG_INF = -1e30

LANES = 128
SUBLANES = 8
VMEM_LIMIT = 56 * 1024 * 1024

TOKEN_TILE = 512
FF_CHUNK = 256
GDN_ROWS = 256
SAMPLE_SLOT = 8
SAMPLE_GROUP = 16
PAGES_PER_STEP = 16


def _dot(a, b, precision=None):
    return jnp.dot(a, b, preferred_element_type=F32, precision=precision)


def _dot_nt(a, b, precision=None):
    return lax.dot_general(a, b, (((1,), (1,)), ((), ())), preferred_element_type=F32, precision=precision)


def _dot_tn(a, b, precision=None):
    return lax.dot_general(a, b, (((0,), (0,)), ((), ())), preferred_element_type=F32, precision=precision)


def _rms(x, w_row):
    ms = jnp.mean(x * x, axis=-1, keepdims=True)
    return x * lax.rsqrt(ms + RMS_EPS) * w_row


def _silu(x):
    return x * jax.nn.sigmoid(x)


def _softplus(x):
    return jnp.maximum(x, 0.0) + jnp.log1p(jnp.exp(-jnp.abs(x)))


def _const_spec(shape):
    return pl.BlockSpec(shape, lambda *_: (0,) * len(shape), pipeline_mode=pl.Buffered(1))


def _params(semantics):
    return pltpu.CompilerParams(dimension_semantics=semantics, vmem_limit_bytes=VMEM_LIMIT)


def _swiglu_residual(x, npre, wg_ref, wu_ref, wo_ref, npost):
    xn = _rms(x, npre).astype(BF16)
    d_ff = wg_ref.shape[1]
    acc = jnp.zeros(x.shape, F32)
    for c0 in range(0, d_ff, FF_CHUNK):
        gate = _dot(xn, wg_ref[:, c0:c0 + FF_CHUNK])
        up = _dot(xn, wu_ref[:, c0:c0 + FF_CHUNK])
        act = (_silu(gate) * up).astype(BF16)
        acc = acc + _dot(act, wo_ref[c0:c0 + FF_CHUNK, :])
    return x + 0.5 * _rms(acc, npost)


def _ffn_body(x_ref, npre_ref, wg_ref, wu_ref, wo_ref, npost_ref, o_ref):
    o_ref[...] = _swiglu_residual(x_ref[...], npre_ref[...], wg_ref, wu_ref, wo_ref, npost_ref[...])


def _mix_ffn_body(h_ref, oa_ref, ob_ref, wma_ref, wmb_ref, nmix_ref,
                  npre_ref, wg_ref, wu_ref, wo_ref, npost_ref, o_ref):
    mix = _dot(oa_ref[...].astype(BF16), wma_ref[...]) + _dot(ob_ref[...].astype(BF16), wmb_ref[...])
    h2 = h_ref[...] + _rms(mix, nmix_ref[...])
    o_ref[...] = _swiglu_residual(h2, npre_ref[...], wg_ref, wu_ref, wo_ref, npost_ref[...])


def _token_tile(n):
    return TOKEN_TILE if n % TOKEN_TILE == 0 else n


def _ffn_call(x, npre, wg, wu, wo, npost):
    n, d = x.shape
    tm = _token_tile(n)
    row = pl.BlockSpec((tm, d), lambda i: (i, 0))
    return pl.pallas_call(
        _ffn_body,
        out_shape=jax.ShapeDtypeStruct((n, d), F32),
        grid=(n // tm,),
        in_specs=[row, _const_spec(npre.shape), _const_spec(wg.shape), _const_spec(wu.shape),
                  _const_spec(wo.shape), _const_spec(npost.shape)],
        out_specs=row,
        compiler_params=_params(("parallel",)),
        name="ffn1",
    )(x, npre, wg, wu, wo, npost)


def _mix_ffn_call(h, oa, ob, wma, wmb, nmix, npre, wg, wu, wo, npost):
    n, d = h.shape
    tm = _token_tile(n)
    row = pl.BlockSpec((tm, d), lambda i: (i, 0))
    half = pl.BlockSpec((tm, oa.shape[1]), lambda i: (i, 0))
    consts = [wma, wmb, nmix, npre, wg, wu, wo, npost]
    return pl.pallas_call(
        _mix_ffn_body,
        out_shape=jax.ShapeDtypeStruct((n, d), F32),
        grid=(n // tm,),
        in_specs=[row, half, half] + [_const_spec(c.shape) for c in consts],
        out_specs=row,
        compiler_params=_params(("parallel",)),
        name="mix_ffn2",
    )(h, oa, ob, *consts)


def _proj_body(widths, h_ref, nrm_ref, w_ref, wba_ref, wbat_ref, *out_refs):
    a = _rms(h_ref[...], nrm_ref[...]).astype(BF16)
    c0 = 0
    for ref, width in zip(out_refs[:len(widths)], widths):
        ref[...] = _dot(a, w_ref[:, c0:c0 + width])
        c0 += width
    ba_ref, bat_ref = out_refs[len(widths):]
    ba_ref[...] = _dot(a, wba_ref[...])
    bat_ref[...] = _dot_nt(wbat_ref[...], a)


def _proj_call(h, nrm, w_main, w_ba, w_bat, widths):
    n, d = h.shape
    tm = _token_tile(n)
    outs = [jax.ShapeDtypeStruct((n, w), F32) for w in widths]
    outs += [jax.ShapeDtypeStruct((n, LANES), F32), jax.ShapeDtypeStruct((SUBLANES, n), F32)]
    out_specs = [pl.BlockSpec((tm, w), lambda i: (i, 0)) for w in widths]
    out_specs += [pl.BlockSpec((tm, LANES), lambda i: (i, 0)), pl.BlockSpec((SUBLANES, tm), lambda i: (0, i))]
    return pl.pallas_call(
        functools.partial(_proj_body, tuple(widths)),
        out_shape=outs,
        grid=(n // tm,),
        in_specs=[pl.BlockSpec((tm, d), lambda i: (i, 0)), _const_spec(nrm.shape), _const_spec(w_main.shape),
                  _const_spec(w_ba.shape), _const_spec(w_bat.shape)],
        out_specs=out_specs,
        compiler_params=_params(("parallel",)),
        name="mix_in_proj",
    )(h, nrm, w_main, w_ba, w_bat)


def _top_blocks(scores, valid, n_cand):
    col = lax.broadcasted_iota(jnp.int32, scores.shape, 1)
    s = jnp.where(valid, scores, NEG_INF)
    rank = jnp.zeros(scores.shape, jnp.int32)
    for m in range(n_cand):
        sm = s[:, m:m + 1]
        ahead = (sm > s) | ((sm == s) & (col > m))
        rank = rank + jnp.where(ahead, 1, 0)
    return valid & (rank < MOBA_TOPK)


def _moba_prompt_body(head_dim, q_ref, k_ref, v_ref, o_ref, kbf_ref, vbf_ref, kmean_ref):
    j = pl.program_id(2)
    seq = k_ref.shape[0]
    n_blocks = seq // MOBA_BLOCK
    scale = head_dim ** -0.5

    @pl.when(j == 0)
    def _():
        kbf_ref[...] = k_ref[...].astype(BF16)
        vbf_ref[...] = v_ref[...].astype(BF16)
        kmean_ref[...] = jnp.zeros(kmean_ref.shape, F32)
        for n in range(n_blocks):
            blk = k_ref[n * MOBA_BLOCK:(n + 1) * MOBA_BLOCK, :]
            kmean_ref[n:n + 1, :] = jnp.mean(blk, axis=0, keepdims=True)

    q = q_ref[...]
    lane = lax.broadcasted_iota(jnp.int32, q.shape, 1)
    col = lax.broadcasted_iota(jnp.int32, (MOBA_BLOCK, LANES), 1)
    row_i = lax.broadcasted_iota(jnp.int32, (MOBA_BLOCK, MOBA_BLOCK), 0)
    col_i = lax.broadcasted_iota(jnp.int32, (MOBA_BLOCK, MOBA_BLOCK), 1)
    causal = col_i <= row_i
    kmean = kmean_ref[...].astype(BF16)
    own0 = pl.multiple_of(j * MOBA_BLOCK, MOBA_BLOCK)
    k_own = kbf_ref[pl.ds(own0, MOBA_BLOCK), :]
    v_own = vbf_ref[pl.ds(own0, MOBA_BLOCK), :]

    outs = []
    for hh in range(LANES // head_dim):
        in_head = (lane >= hh * head_dim) & (lane < (hh + 1) * head_dim)
        qh = jnp.where(in_head, q, 0.0).astype(BF16)
        gate = _dot_nt(qh, kmean)
        sel = _top_blocks(gate, col < j, n_blocks)
        self_f = jnp.where(sel, 1.0, 0.0)

        logits = jnp.where(causal, _dot_nt(qh, k_own) * scale, NEG_INF)
        m0 = jnp.max(logits, axis=-1, keepdims=True)
        p0 = jnp.exp(logits - m0)
        l0 = jnp.sum(p0, axis=-1, keepdims=True)
        acc0 = _dot(p0.astype(BF16), v_own)

        def past_block(n, carry, qh=qh, self_f=self_f):
            m_run, l_run, acc = carry
            r0 = pl.multiple_of(n * MOBA_BLOCK, MOBA_BLOCK)
            kb = kbf_ref[pl.ds(r0, MOBA_BLOCK), :]
            vb = vbf_ref[pl.ds(r0, MOBA_BLOCK), :]
            picked = jnp.sum(jnp.where(col == n, self_f, 0.0), axis=-1, keepdims=True) > 0.5
            lg = jnp.where(picked, _dot_nt(qh, kb) * scale, NEG_INF)
            m_new = jnp.maximum(m_run, jnp.max(lg, axis=-1, keepdims=True))
            alpha = jnp.exp(m_run - m_new)
            p = jnp.exp(lg - m_new)
            l_new = alpha * l_run + jnp.sum(p, axis=-1, keepdims=True)
            acc_new = alpha * acc + _dot(p.astype(BF16), vb)
            return m_new, l_new, acc_new

        _, l_fin, acc_fin = lax.fori_loop(0, j, past_block, (m0, l0, acc0))
        outs.append((in_head, acc_fin / l_fin))

    out = jnp.zeros(q.shape, F32)
    for in_head, val in outs:
        out = jnp.where(in_head, val, out)
    o_ref[...] = out


def _moba_prompt_call(q, k, v, batch, seq, head_dim):
    n, width = q.shape
    pairs = width // LANES
    nb = seq // MOBA_BLOCK
    qspec = pl.BlockSpec((MOBA_BLOCK, LANES), lambda b, hp, j: (b * nb + j, hp))
    kvspec = pl.BlockSpec((seq, LANES), lambda b, hp, j: (b, hp))
    return pl.pallas_call(
        functools.partial(_moba_prompt_body, head_dim),
        out_shape=jax.ShapeDtypeStruct((n, width), F32),
        grid=(batch, pairs, nb),
        in_specs=[qspec, kvspec, kvspec],
        out_specs=qspec,
        scratch_shapes=[pltpu.VMEM((seq, LANES), BF16), pltpu.VMEM((seq, LANES), BF16),
                        pltpu.VMEM((LANES, LANES), F32)],
        compiler_params=_params(("parallel", "parallel", "arbitrary")),
        name="moba_prompt",
    )(q, k, v)


def _moba_sample_body(n_heads, head_dim, n_new, pt_ref, q_ref, kn_ref, vn_ref, *refs):
    pp = PAGES_PER_STEP
    k_refs = refs[:pp]
    v_refs = refs[pp:2 * pp]
    o_ref = refs[2 * pp]
    qm_ref, logit_ref, ksum_ref, p_ref, pnew_ref, linv_ref, acc_ref = refs[2 * pp + 1:]
    j = pl.program_id(1)
    n_phase = pl.num_programs(1) // 2
    page = k_refs[0].shape[1]
    width = n_heads * head_dim
    rows = n_new * n_heads
    pages_per_block = MOBA_BLOCK // page
    blocks_per_step = pp // pages_per_block
    n_past_blocks = n_phase * blocks_per_step
    span = pp * page

    @pl.when(j == 0)
    def _():
        q = q_ref[0] * (head_dim ** -0.5)
        lane = lax.broadcasted_iota(jnp.int32, (n_heads, width), 1)
        head = lax.broadcasted_iota(jnp.int32, (n_heads, width), 0)
        in_head = (lane >= head * head_dim) & (lane < (head + 1) * head_dim)
        for t in range(n_new):
            qt = jnp.broadcast_to(q[t:t + 1, :], (n_heads, width))
            qm_ref[t * n_heads:(t + 1) * n_heads, :] = jnp.where(in_head, qt, 0.0).astype(BF16)
        acc_ref[...] = jnp.zeros(acc_ref.shape, F32)

    @pl.when(j < n_phase)
    def _():
        qm = qm_ref[...]
        for i in range(pp):
            kp = k_refs[i][0]
            logit_ref[j, :, i * page:(i + 1) * page] = _dot_nt(qm, kp.astype(BF16))
            part = jnp.sum(kp, axis=0, keepdims=True)
            b = i // pages_per_block
            if i % pages_per_block == 0:
                ksum_ref[j, b:b + 1, :] = part
            else:
                ksum_ref[j, b:b + 1, :] = ksum_ref[j, b:b + 1, :] + part

    @pl.when(j == n_phase - 1)
    def _():
        qm = qm_ref[...]
        kmean = jnp.concatenate([ksum_ref[s] for s in range(n_phase)], axis=0) * (1.0 / MOBA_BLOCK)
        kmean = jnp.concatenate([kmean, jnp.zeros((LANES - n_past_blocks, width), F32)], axis=0)
        gate = _dot_nt(qm, kmean.astype(BF16))
        col = lax.broadcasted_iota(jnp.int32, (rows, LANES), 1)
        sel = _top_blocks(gate, col < n_past_blocks, n_past_blocks)
        self_f = jnp.where(sel, 1.0, 0.0)

        kn = jnp.concatenate([kn_ref[0], jnp.zeros((LANES - SAMPLE_SLOT, width), F32)], axis=0)
        row = lax.broadcasted_iota(jnp.int32, (rows, LANES), 0)
        new_ok = (col * n_heads <= row) & (col < n_new)
        lg_new = jnp.where(new_ok, _dot_nt(qm, kn.astype(BF16)), NEG_INF)
        m = jnp.max(lg_new, axis=-1, keepdims=True)
        masks = []
        for b in range(n_past_blocks):
            picked = self_f[:, b:b + 1] > 0.5
            masks.append(picked)
            s, o = divmod(b, blocks_per_step)
            lg = logit_ref[s, :, o * MOBA_BLOCK:(o + 1) * MOBA_BLOCK]
            m = jnp.maximum(m, jnp.max(jnp.where(picked, lg, NEG_INF), axis=-1, keepdims=True))
        p_new = jnp.exp(lg_new - m)
        l = jnp.sum(p_new, axis=-1, keepdims=True)
        pnew_ref[...] = p_new.astype(BF16)
        for b in range(n_past_blocks):
            s, o = divmod(b, blocks_per_step)
            lg = logit_ref[s, :, o * MOBA_BLOCK:(o + 1) * MOBA_BLOCK]
            p = jnp.exp(jnp.where(masks[b], lg, NEG_INF) - m)
            l = l + jnp.sum(p, axis=-1, keepdims=True)
            p_ref[s, :, o * MOBA_BLOCK:(o + 1) * MOBA_BLOCK] = p.astype(BF16)
        linv_ref[...] = 1.0 / l

    @pl.when(j >= n_phase)
    def _():
        acc = acc_ref[...]
        for i in range(pp):
            vp = v_refs[i][0]
            acc = acc + _dot(p_ref[j - n_phase, :, i * page:(i + 1) * page], vp.astype(BF16))
        acc_ref[...] = acc

    @pl.when(j == 2 * n_phase - 1)
    def _():
        vn = jnp.concatenate([vn_ref[0], jnp.zeros((LANES - SAMPLE_SLOT, width), F32)], axis=0)
        acc = (acc_ref[...] + _dot(pnew_ref[...], vn.astype(BF16))) * linv_ref[...]
        lane = lax.broadcasted_iota(jnp.int32, (n_heads, width), 1)
        head = lax.broadcasted_iota(jnp.int32, (n_heads, width), 0)
        in_head = (lane >= head * head_dim) & (lane < (head + 1) * head_dim)
        for t in range(n_new):
            rows_t = acc[t * n_heads:(t + 1) * n_heads, :]
            o_ref[0, t:t + 1, :] = jnp.sum(jnp.where(in_head, rows_t, 0.0), axis=0, keepdims=True)


def _moba_sample_call(q, kn, vn, pool_k, pool_v, page_table, n_heads, head_dim):
    seqs, n_new, width = q.shape
    page = pool_k.shape[1]
    n_pages = page_table.shape[1]
    pp = PAGES_PER_STEP
    n_phase = n_pages // pp
    assert n_pages % pp == 0 and (pp * page) % MOBA_BLOCK == 0 and MOBA_BLOCK % page == 0
    assert n_pages * page // MOBA_BLOCK <= LANES and n_new <= SUBLANES
    rows = n_new * n_heads
    span = pp * page

    new_spec = pl.BlockSpec((1, n_new, width), lambda s, j, pt: (s, 0, 0))
    slot_spec = pl.BlockSpec((1, SAMPLE_SLOT, width), lambda s, j, pt: (s, 0, 0))

    def k_spec(i):
        return pl.BlockSpec((1, page, width),
                            lambda s, j, pt: (pt[s, jnp.minimum(j, n_phase - 1) * pp + i], 0, 0))

    def v_spec(i):
        return pl.BlockSpec((1, page, width),
                            lambda s, j, pt: (pt[s, jnp.maximum(j - n_phase, 0) * pp + i], 0, 0))

    grid_spec = pltpu.PrefetchScalarGridSpec(
        num_scalar_prefetch=1,
        grid=(seqs, 2 * n_phase),
        in_specs=[new_spec, slot_spec, slot_spec] + [k_spec(i) for i in range(pp)] + [v_spec(i) for i in range(pp)],
        out_specs=new_spec,
        scratch_shapes=[pltpu.VMEM((rows, width), BF16),
                        pltpu.VMEM((n_phase, rows, span), F32),
                        pltpu.VMEM((n_phase, SUBLANES, width), F32),
                        pltpu.VMEM((n_phase, rows, span), BF16),
                        pltpu.VMEM((rows, LANES), BF16),
                        pltpu.VMEM((rows, 1), F32),
                        pltpu.VMEM((rows, width), F32)],
    )
    assert span // MOBA_BLOCK == SUBLANES
    return pl.pallas_call(
        functools.partial(_moba_sample_body, n_heads, head_dim, n_new),
        out_shape=jax.ShapeDtypeStruct((seqs, n_new, width), F32),
        grid_spec=grid_spec,
        compiler_params=_params(("parallel", "arbitrary")),
        name="moba_sample",
    )(page_table, q, kn, vn, *([pool_k] * pp), *([pool_v] * pp))


def _unit_lower_inverse(l_strict, chunk):
    n = l_strict.shape[0]
    eye = jnp.where(lax.broadcasted_iota(jnp.int32, (n, n), 0) == lax.broadcasted_iota(jnp.int32, (n, n), 1),
                    1.0, 0.0)
    hi = lax.Precision.HIGHEST
    m = -l_strict
    s = eye + m
    p = _dot(m, m, hi)
    steps = chunk.bit_length() - 2
    for it in range(steps):
        s = s + _dot(p, s, hi)
        if it + 1 < steps:
            p = _dot(p, p, hi)
    return s


def _gdn_prepare(xs_ref, base, rows, convw_ref, ba_ref, bat_ref, prow_ref, pcol_ref, chunk, n_heads, hd, valid):
    w = convw_ref[...]
    taps = w.shape[0]
    y = xs_ref[pl.ds(base, rows), :] * w[0:1, :]
    for t in range(1, taps):
        y = y + xs_ref[pl.ds(base + t, rows), :] * w[t:t + 1, :]
    y = _silu(y)
    bw = n_heads * hd
    ba = ba_ref[...]
    bat = bat_ref[...]
    beta_all = jax.nn.sigmoid(ba)
    g_all = -jnp.exp(prow_ref[0:1, :]) * _softplus(ba + prow_ref[1:2, :])
    g_rows = -jnp.exp(pcol_ref[:, 0:1]) * _softplus(bat + pcol_ref[:, 1:2])
    if valid is not None:
        vcol, vrow = valid
        beta_all = jnp.where(vcol, beta_all, 0.0)
        g_all = jnp.where(vcol, g_all, 0.0)
        g_rows = jnp.where(vrow, g_rows, 0.0)
    ri = lax.broadcasted_iota(jnp.int32, (rows, rows), 0)
    ci = lax.broadcasted_iota(jnp.int32, (rows, rows), 1)
    shift = chunk.bit_length() - 1
    same = (ri >> shift) == (ci >> shift)
    lower = jnp.where(same & (ci <= ri), 1.0, 0.0)
    upper = jnp.where(same & (ri <= ci), 1.0, 0.0)
    hi = lax.Precision.HIGHEST
    gc_cols = _dot(lower, g_all, hi)
    gc_rows = _dot(g_rows, upper, hi)
    heads = []
    for h in range(n_heads):
        q = y[:, h * hd:(h + 1) * hd]
        k = y[:, bw + h * hd:bw + (h + 1) * hd]
        v = y[:, 2 * bw + h * hd:2 * bw + (h + 1) * hd]
        q = q * lax.rsqrt(jnp.sum(q * q, axis=-1, keepdims=True) + L2_EPS) * (hd ** -0.5)
        k = k * lax.rsqrt(jnp.sum(k * k, axis=-1, keepdims=True) + L2_EPS)
        if valid is not None:
            k = jnp.where(valid[0], k, 0.0)
            v = jnp.where(valid[0], v, 0.0)
        beta = beta_all[:, h:h + 1]
        gcc = gc_cols[:, n_heads + h:n_heads + h + 1]
        gcr = gc_rows[n_heads + h:n_heads + h + 1, :]
        heads.append((q, k, v, beta, gcc, gcr))
    return heads, same, ri, ci


def _gdn_head_matrices(q, k, v, beta, gcc, gcr, same, ri, ci, chunk):
    causal = same & (ci <= ri)
    strict = same & (ci < ri)
    decay = jnp.where(causal, jnp.exp(jnp.where(causal, gcc - gcr, 0.0)), 0.0)
    kb = k * beta
    kb16 = kb.astype(BF16)
    k16 = k.astype(BF16)
    lmat = jnp.where(strict, _dot_nt(kb16, k16) * decay, 0.0)
    attn = _dot_nt(q.astype(BF16), k16) * decay
    tinv = _unit_lower_inverse(lmat, chunk)
    egc = jnp.exp(gcc)
    rhs = jnp.concatenate([v * beta, kb * egc], axis=1).astype(BF16)
    uw = _dot(tinv.astype(BF16), rhs)
    hd = q.shape[1]
    return uw[:, :hd], uw[:, hd:], attn.astype(BF16), q * egc


def _gated_out(o, gnorm_row, z):
    return _rms(o, gnorm_row) * _silu(z)


def _gdn_prompt_body(n_heads, hd, qkv_ref, z_ref, ba_ref, bat_ref, convw_ref, prow_ref, pcol_ref, gnorm_ref,
                     o_ref, sfin_ref, xs_ref, s_ref, vn_ref):
    i = pl.program_id(1)
    rows = qkv_ref.shape[0]
    chunk = GDN_CHUNK

    @pl.when(i == 0)
    def _():
        xs_ref[0:SUBLANES, :] = jnp.zeros((SUBLANES, xs_ref.shape[1]), F32)
        s_ref[...] = jnp.zeros(s_ref.shape, F32)

    x = qkv_ref[...]
    xs_ref[SUBLANES:SUBLANES + rows, :] = x
    taps = convw_ref.shape[0]
    heads, same, ri, ci = _gdn_prepare(xs_ref, SUBLANES - taps + 1, rows, convw_ref, ba_ref, bat_ref, prow_ref,
                                       pcol_ref, chunk, n_heads, hd, None)
    xs_ref[0:SUBLANES, :] = x[rows - SUBLANES:rows, :]

    gnorm = gnorm_ref[...]
    for h, (q, k, v, beta, gcc, gcr) in enumerate(heads):
        u, w, attn16, qg = _gdn_head_matrices(q, k, v, beta, gcc, gcr, same, ri, ci, chunk)
        vn_ref[...] = jnp.zeros(vn_ref.shape, BF16)
        s = s_ref[h]
        for c in range(rows // chunk):
            r0 = c * chunk
            s16 = s.astype(BF16)
            v_new = u[r0:r0 + chunk] - _dot(w[r0:r0 + chunk].astype(BF16), s16)
            vn_ref[r0:r0 + chunk, :] = v_new.astype(BF16)
            o = _dot(qg[r0:r0 + chunk].astype(BF16), s16) + _dot(attn16[r0:r0 + chunk, :], vn_ref[...])
            gl = gcc[r0 + chunk - 1:r0 + chunk, :]
            kd = (k[r0:r0 + chunk] * jnp.exp(gl - gcc[r0:r0 + chunk])).astype(BF16)
            s = s * jnp.exp(gl) + _dot_tn(kd, v_new.astype(BF16))
            zc = z_ref[r0:r0 + chunk, h * hd:(h + 1) * hd]
            o_ref[r0:r0 + chunk, h * hd:(h + 1) * hd] = _gated_out(o, gnorm, zc)
        s_ref[h] = s

    @pl.when(i == pl.num_programs(1) - 1)
    def _():
        sfin_ref[0] = s_ref[...]


def _gdn_prompt_call(qkv, z, ba, bat, convw, prow, pcol, gnorm, batch, seq, n_heads, hd):
    n, cdim = qkv.shape
    rows = GDN_ROWS
    nblk = seq // rows
    bw = n_heads * hd
    tile = lambda width: pl.BlockSpec((rows, width), lambda b, i: (b * nblk + i, 0))
    return pl.pallas_call(
        functools.partial(_gdn_prompt_body, n_heads, hd),
        out_shape=[jax.ShapeDtypeStruct((n, bw), F32), jax.ShapeDtypeStruct((batch, n_heads, hd, hd), F32)],
        grid=(batch, nblk),
        in_specs=[tile(cdim), tile(bw), tile(LANES),
                  pl.BlockSpec((SUBLANES, rows), lambda b, i: (0, b * nblk + i)),
                  _const_spec(convw.shape), _const_spec(prow.shape), _const_spec(pcol.shape),
                  _const_spec(gnorm.shape)],
        out_specs=[tile(bw), pl.BlockSpec((1, n_heads, hd, hd), lambda b, i: (b, 0, 0, 0))],
        scratch_shapes=[pltpu.VMEM((SUBLANES + rows, cdim), F32), pltpu.VMEM((n_heads, hd, hd), F32),
                        pltpu.VMEM((rows, hd), BF16)],
        compiler_params=_params(("parallel", "arbitrary")),
        name="gdn_prompt",
    )(qkv, z, ba, bat, convw, prow, pcol, gnorm)


def _gdn_sample_body(n_heads, hd, n_new, xp_ref, z_ref, ba_ref, bat_ref, s0_ref, convw_ref, prow_ref, pcol_ref,
                     gnorm_ref, o_ref, snew_ref, xs_ref):
    rows = xp_ref.shape[0]
    slot = SAMPLE_SLOT
    xs_ref[0:rows, :] = xp_ref[...]
    xs_ref[rows:rows + SUBLANES, :] = jnp.zeros((SUBLANES, xs_ref.shape[1]), F32)
    vcol = (lax.broadcasted_iota(jnp.int32, (rows, 1), 0) & (slot - 1)) < n_new
    vrow = (lax.broadcasted_iota(jnp.int32, (1, rows), 1) & (slot - 1)) < n_new
    heads, same, ri, ci = _gdn_prepare(xs_ref, 0, rows, convw_ref, ba_ref, bat_ref, prow_ref, pcol_ref,
                                       slot, n_heads, hd, (vcol, vrow))
    gnorm = gnorm_ref[...]
    for h, (q, k, v, beta, gcc, gcr) in enumerate(heads):
        u, w, attn16, qg = _gdn_head_matrices(q, k, v, beta, gcc, gcr, same, ri, ci, slot)
        v_parts, qs_parts = [], []
        for g in range(rows // slot):
            r0 = g * slot
            s16 = s0_ref[g, h].astype(BF16)
            lhs = jnp.concatenate([w[r0:r0 + slot], qg[r0:r0 + slot]], axis=0).astype(BF16)
            both = _dot(lhs, s16)
            v_parts.append(u[r0:r0 + slot] - both[:slot])
            qs_parts.append(both[slot:])
        v_new = jnp.concatenate(v_parts, axis=0)
        o = jnp.concatenate(qs_parts, axis=0) + _dot(attn16, v_new.astype(BF16))
        o_ref[:, h * hd:(h + 1) * hd] = _gated_out(o, gnorm, z_ref[:, h * hd:(h + 1) * hd])
        for g in range(rows // slot):
            r0 = g * slot
            gl = gcc[r0 + slot - 1:r0 + slot, :]
            kd = k[r0:r0 + slot] * jnp.exp(gl - gcc[r0:r0 + slot])
            snew_ref[g, h] = s0_ref[g, h] * jnp.exp(gl) + _dot_tn(kd, v_new[r0:r0 + slot])


def _gdn_sample_call(xp, z, ba, bat, s0, convw, prow, pcol, gnorm, n_heads, hd, n_new):
    n, cdim = xp.shape
    rows = SAMPLE_GROUP * SAMPLE_SLOT
    bw = n_heads * hd
    tile = lambda width: pl.BlockSpec((rows, width), lambda i: (i, 0))
    sspec = pl.BlockSpec((SAMPLE_GROUP, n_heads, hd, hd), lambda i: (i, 0, 0, 0))
    return pl.pallas_call(
        functools.partial(_gdn_sample_body, n_heads, hd, n_new),
        out_shape=[jax.ShapeDtypeStruct((n, bw), F32), jax.ShapeDtypeStruct(s0.shape, F32)],
        grid=(n // rows,),
        in_specs=[tile(cdim), tile(bw), tile(LANES), pl.BlockSpec((SUBLANES, rows), lambda i: (0, i)), sspec,
                  _const_spec(convw.shape), _const_spec(prow.shape), _const_spec(pcol.shape),
                  _const_spec(gnorm.shape)],
        out_specs=[tile(bw), sspec],
        scratch_shapes=[pltpu.VMEM((rows + SUBLANES, cdim), F32)],
        compiler_params=_params(("parallel",)),
        name="gdn_sample",
    )(xp, z, ba, bat, s0, convw, prow, pcol, gnorm)


def _row(v):
    return v.reshape(1, -1).astype(F32)


def kernel(x_prompt, x_sample, cache_k, cache_v, state_gdn, state_conv, page_table, norm_ffn1_pre, norm_ffn1_post, w_ffn1_in, w_ffn1_out, norm_mix_pre, norm_mix_post, w_mix_in, w_mix_out, gdn_conv_w, gdn_a_log, gdn_dt_bias, gdn_norm_w, norm_ffn2_pre, norm_ffn2_post, w_ffn2_in, w_ffn2_out):
    batch, seq, d_model = x_prompt.shape
    dec_batch, dec_seq, _ = x_sample.shape
    depth = w_ffn1_in.shape[0]
    d_ff = w_ffn1_out.shape[1]
    a_heads, a_hd = cache_k.shape[3], cache_k.shape[4]
    b_heads, b_hd = state_gdn.shape[2], state_gdn.shape[3]
    aw, bw = a_heads * a_hd, b_heads * b_hd
    conv_dim = gdn_conv_w.shape[2]
    taps = gdn_conv_w.shape[1]
    widths = (aw, aw, aw, conv_dim, bw)
    main = sum(widths)
    assert w_mix_in.shape[2] == main + 2 * b_heads and 2 * b_heads <= SUBLANES
    assert seq % GDN_ROWS == 0 and seq % MOBA_BLOCK == 0 and dec_batch % SAMPLE_GROUP == 0
    assert taps - 1 + dec_seq <= SAMPLE_SLOT and dec_seq >= taps - 1 and seq >= taps - 1
    pad = SAMPLE_SLOT - (taps - 1) - dec_seq

    yp = x_prompt.reshape(batch * seq, d_model)
    ys = x_sample.reshape(dec_batch * dec_seq, d_model)
    outs = [[] for _ in range(8)]
    for l in range(depth):
        wg1, wu1 = w_ffn1_in[l, :, :d_ff].astype(BF16), w_ffn1_in[l, :, d_ff:].astype(BF16)
        wo1 = w_ffn1_out[l].astype(BF16)
        wg2, wu2 = w_ffn2_in[l, :, :d_ff].astype(BF16), w_ffn2_in[l, :, d_ff:].astype(BF16)
        wo2 = w_ffn2_out[l].astype(BF16)
        w_main = w_mix_in[l, :, :main].astype(BF16)
        w_tail = w_mix_in[l, :, main:]
        w_ba = jnp.pad(w_tail, ((0, 0), (0, LANES - 2 * b_heads))).astype(BF16)
        w_bat = jnp.pad(w_tail.T, ((0, SUBLANES - 2 * b_heads), (0, 0))).astype(BF16)
        wma, wmb = w_mix_out[l, :aw].astype(BF16), w_mix_out[l, aw:].astype(BF16)
        a_log, dt_bias = gdn_a_log[l].astype(F32), gdn_dt_bias[l].astype(F32)
        prow = jnp.zeros((SUBLANES, LANES), F32)
        prow = prow.at[0, b_heads:2 * b_heads].set(a_log).at[1, b_heads:2 * b_heads].set(dt_bias)
        pcol = jnp.zeros((SUBLANES, LANES), F32)
        pcol = pcol.at[b_heads:2 * b_heads, 0].set(a_log).at[b_heads:2 * b_heads, 1].set(dt_bias)
        convw = gdn_conv_w[l].astype(F32)
        gnorm = _row(gdn_norm_w[l])
        pool_k = cache_k[l].reshape(cache_k.shape[1], cache_k.shape[2], aw)
        pool_v = cache_v[l].reshape(cache_v.shape[1], cache_v.shape[2], aw)

        def trunk_in(x):
            h = _ffn_call(x, _row(norm_ffn1_pre[l]), wg1, wu1, wo1, _row(norm_ffn1_post[l]))
            return (h,) + tuple(_proj_call(h, _row(norm_mix_pre[l]), w_main, w_ba, w_bat, widths))

        def trunk_out(h, oa, ob):
            return _mix_ffn_call(h, oa, ob, wma, wmb, _row(norm_mix_post[l]), _row(norm_ffn2_pre[l]),
                                 wg2, wu2, wo2, _row(norm_ffn2_post[l]))

        hp, qa, ka, va, qkv, z, ba, bat = trunk_in(yp)
        oa = _moba_prompt_call(qa, ka, va, batch, seq, a_hd)
        ob, s_p = _gdn_prompt_call(qkv, z, ba, bat, convw, prow, pcol, gnorm, batch, seq, b_heads, b_hd)
        yp = trunk_out(hp, oa, ob)
        outs[0].append(ka.reshape(batch, seq, a_heads, a_hd))
        outs[1].append(va.reshape(batch, seq, a_heads, a_hd))
        outs[4].append(s_p)
        outs[6].append(qkv.reshape(batch, seq, conv_dim)[:, seq - (taps - 1):, :])

        hs, qa, ka, va, qkv, z, ba, bat = trunk_in(ys)
        seq3 = lambda a: a.reshape(dec_batch, dec_seq, a.shape[-1])
        slots = lambda a: jnp.pad(seq3(a), ((0, 0), (0, SAMPLE_SLOT - dec_seq), (0, 0)))
        oa = _moba_sample_call(seq3(qa), slots(ka), slots(va), pool_k, pool_v, page_table, a_heads, a_hd)
        qkv3 = seq3(qkv)
        slot_rows = lambda a: slots(a).reshape(dec_batch * SAMPLE_SLOT, a.shape[-1])
        xp = jnp.concatenate([state_conv[l].astype(F32), qkv3, jnp.zeros((dec_batch, pad, conv_dim), F32)],
                             axis=1).reshape(dec_batch * SAMPLE_SLOT, conv_dim)
        bat_slots = jnp.pad(bat.reshape(SUBLANES, dec_batch, dec_seq),
                            ((0, 0), (0, 0), (0, SAMPLE_SLOT - dec_seq))).reshape(SUBLANES, dec_batch * SAMPLE_SLOT)
        ob_slots, s_s = _gdn_sample_call(xp, slot_rows(z), slot_rows(ba), bat_slots, state_gdn[l].astype(F32),
                                         convw, prow, pcol, gnorm, b_heads, b_hd, dec_seq)
        ob = ob_slots.reshape(dec_batch, SAMPLE_SLOT, bw)[:, :dec_seq].reshape(dec_batch * dec_seq, bw)
        ys = trunk_out(hs, oa.reshape(dec_batch * dec_seq, aw), ob)
        outs[2].append(ka.reshape(dec_batch, dec_seq, a_heads, a_hd))
        outs[3].append(va.reshape(dec_batch, dec_seq, a_heads, a_hd))
        outs[5].append(s_s)
        outs[7].append(jnp.concatenate([state_conv[l].astype(F32), qkv3], axis=1)[:, dec_seq:, :])

    stacked = [jnp.stack(o) for o in outs]
    return (yp.reshape(batch, seq, d_model), ys.reshape(dec_batch, dec_seq, d_model), *stacked)
```

```python
import functools

import jax
import jax.numpy as jnp
from jax import lax
from jax.experimental import pallas as pl
from jax.experimental.pallas import tpu as pltpu

F32 = jnp.float32
BF16 = jnp.bfloat16

MOBA_BLOCK = 256
MOBA_TOPK = 3
GDN_CHUNK = 64
RMS_EPS = 1e-6
L2_EPS = 1e-6
NEG_INF = -1e30

LANES = 128
SUBLANES = 8
VMEM_LIMIT = 56 * 1024 * 1024

TOKEN_TILE = 512
FF_CHUNK = 256
GDN_ROWS = 256
SAMPLE_SLOT = 8
SAMPLE_GROUP = 16
PAGES_PER_CHUNK = 8


def _dot(a, b, precision=None):
    return jnp.dot(a, b, preferred_element_type=F32, precision=precision)


def _dot_nt(a, b, precision=None):
    return lax.dot_general(a, b, (((1,), (1,)), ((), ())), preferred_element_type=F32, precision=precision)


def _dot_tn(a, b, precision=None):
    return lax.dot_general(a, b, (((0,), (0,)), ((), ())), preferred_element_type=F32, precision=precision)


def _rms(x, w_row):
    ms = jnp.mean(x * x, axis=-1, keepdims=True)
    return x * lax.rsqrt(ms + RMS_EPS) * w_row


def _silu(x):
    return x * jax.nn.sigmoid(x)


def _softplus(x):
    return jnp.maximum(x, 0.0) + jnp.log1p(jnp.exp(-jnp.abs(x)))


def _const_spec(shape):
    return pl.BlockSpec(shape, lambda *_: (0,) * len(shape), pipeline_mode=pl.Buffered(1))


def _params(semantics):
    return pltpu.CompilerParams(dimension_semantics=semantics, vmem_limit_bytes=VMEM_LIMIT)


def _swiglu_residual(x, npre, wg_ref, wu_ref, wo_ref, npost):
    xn = _rms(x, npre).astype(BF16)
    d_ff = wg_ref.shape[1]
    acc = jnp.zeros(x.shape, F32)
    for c0 in range(0, d_ff, FF_CHUNK):
        gate = _dot(xn, wg_ref[:, c0:c0 + FF_CHUNK])
        up = _dot(xn, wu_ref[:, c0:c0 + FF_CHUNK])
        act = (_silu(gate) * up).astype(BF16)
        acc = acc + _dot(act, wo_ref[c0:c0 + FF_CHUNK, :])
    return x + 0.5 * _rms(acc, npost)


def _ffn_body(x_ref, npre_ref, wg_ref, wu_ref, wo_ref, npost_ref, o_ref):
    o_ref[...] = _swiglu_residual(x_ref[...], npre_ref[...], wg_ref, wu_ref, wo_ref, npost_ref[...])


def _mix_ffn_body(h_ref, oa_ref, ob_ref, wma_ref, wmb_ref, nmix_ref,
                  npre_ref, wg_ref, wu_ref, wo_ref, npost_ref, o_ref):
    mix = _dot(oa_ref[...].astype(BF16), wma_ref[...]) + _dot(ob_ref[...].astype(BF16), wmb_ref[...])
    h2 = h_ref[...] + _rms(mix, nmix_ref[...])
    o_ref[...] = _swiglu_residual(h2, npre_ref[...], wg_ref, wu_ref, wo_ref, npost_ref[...])


def _token_tile(n):
    return TOKEN_TILE if n % TOKEN_TILE == 0 else n


def _ffn_call(x, npre, wg, wu, wo, npost):
    n, d = x.shape
    tm = _token_tile(n)
    row = pl.BlockSpec((tm, d), lambda i: (i, 0))
    return pl.pallas_call(
        _ffn_body,
        out_shape=jax.ShapeDtypeStruct((n, d), F32),
        grid=(n // tm,),
        in_specs=[row, _const_spec(npre.shape), _const_spec(wg.shape), _const_spec(wu.shape),
                  _const_spec(wo.shape), _const_spec(npost.shape)],
        out_specs=row,
        compiler_params=_params(("parallel",)),
        name="ffn1",
    )(x, npre, wg, wu, wo, npost)


def _mix_ffn_call(h, oa, ob, wma, wmb, nmix, npre, wg, wu, wo, npost):
    n, d = h.shape
    tm = _token_tile(n)
    row = pl.BlockSpec((tm, d), lambda i: (i, 0))
    half = pl.BlockSpec((tm, oa.shape[1]), lambda i: (i, 0))
    consts = [wma, wmb, nmix, npre, wg, wu, wo, npost]
    return pl.pallas_call(
        _mix_ffn_body,
        out_shape=jax.ShapeDtypeStruct((n, d), F32),
        grid=(n // tm,),
        in_specs=[row, half, half] + [_const_spec(c.shape) for c in consts],
        out_specs=row,
        compiler_params=_params(("parallel",)),
        name="mix_ffn2",
    )(h, oa, ob, *consts)


def _proj_body(widths, h_ref, nrm_ref, w_ref, wba_ref, wbat_ref, *out_refs):
    a = _rms(h_ref[...], nrm_ref[...]).astype(BF16)
    c0 = 0
    for ref, width in zip(out_refs[:len(widths)], widths):
        ref[...] = _dot(a, w_ref[:, c0:c0 + width])
        c0 += width
    ba_ref, bat_ref = out_refs[len(widths):]
    ba_ref[...] = _dot(a, wba_ref[...])
    bat_ref[...] = _dot_nt(wbat_ref[...], a)


def _proj_call(h, nrm, w_main, w_ba, w_bat, widths):
    n, d = h.shape
    tm = _token_tile(n)
    outs = [jax.ShapeDtypeStruct((n, w), F32) for w in widths]
    outs += [jax.ShapeDtypeStruct((n, LANES), F32), jax.ShapeDtypeStruct((SUBLANES, n), F32)]
    out_specs = [pl.BlockSpec((tm, w), lambda i: (i, 0)) for w in widths]
    out_specs += [pl.BlockSpec((tm, LANES), lambda i: (i, 0)), pl.BlockSpec((SUBLANES, tm), lambda i: (0, i))]
    return pl.pallas_call(
        functools.partial(_proj_body, tuple(widths)),
        out_shape=outs,
        grid=(n // tm,),
        in_specs=[pl.BlockSpec((tm, d), lambda i: (i, 0)), _const_spec(nrm.shape), _const_spec(w_main.shape),
                  _const_spec(w_ba.shape), _const_spec(w_bat.shape)],
        out_specs=out_specs,
        compiler_params=_params(("parallel",)),
        name="mix_in_proj",
    )(h, nrm, w_main, w_ba, w_bat)


def _top_blocks(scores, valid, n_cand):
    col = lax.broadcasted_iota(jnp.int32, scores.shape, 1)
    s = jnp.where(valid, scores, NEG_INF)
    rank = jnp.zeros(scores.shape, jnp.int32)
    for m in range(n_cand):
        sm = s[:, m:m + 1]
        ahead = (sm > s) | ((sm == s) & (col > m))
        rank = rank + jnp.where(ahead, 1, 0)
    return valid & (rank < MOBA_TOPK)


def _top_blocks_t(scores, valid):
    row = lax.broadcasted_iota(jnp.int32, scores.shape, 0)
    s = jnp.where(valid, scores, NEG_INF)
    rank = jnp.zeros(scores.shape, jnp.int32)
    for m in range(scores.shape[0]):
        sm = s[m:m + 1, :]
        ahead = (sm > s) | ((sm == s) & (row > m))
        rank = rank + jnp.where(ahead, 1, 0)
    return valid & (rank < MOBA_TOPK)


def _moba_prompt_body(head_dim, q_ref, k_ref, v_ref, o_ref, kbf_ref, vt_ref, kmean_ref, sel_ref):
    j = pl.program_id(2)
    n_blocks = vt_ref.shape[0]
    n_heads = LANES // head_dim
    blk = MOBA_BLOCK

    @pl.when(j == 0)
    def _():
        kbf_ref[...] = k_ref[...].astype(BF16)
        for n in range(n_blocks):
            rows = slice(n * blk, (n + 1) * blk)
            vt_ref[n] = v_ref[rows, :].T.astype(BF16)
            kmean_ref[n:n + 1, :] = jnp.mean(k_ref[rows, :], axis=0, keepdims=True)

    q = q_ref[...] * (head_dim ** -0.5)
    lane = lax.broadcasted_iota(jnp.int32, q.shape, 1)
    key_i = lax.broadcasted_iota(jnp.int32, (blk, blk), 0)
    qry_i = lax.broadcasted_iota(jnp.int32, (blk, blk), 1)
    causal = key_i <= qry_i
    cand = lax.broadcasted_iota(jnp.int32, (n_blocks, blk), 0)
    kmean = kmean_ref[...].astype(BF16)
    k_own = kbf_ref[pl.ds(pl.multiple_of(j * blk, blk), blk), :]

    qhs, carry = [], []
    for hh in range(n_heads):
        in_head = (lane >= hh * head_dim) & (lane < (hh + 1) * head_dim)
        qh = jnp.where(in_head, q, 0.0).astype(BF16)
        qhs.append(qh)
        sel = _top_blocks_t(_dot_nt(kmean, qh), cand < j)
        sel_ref[hh] = jnp.where(sel, 1.0, 0.0)
        lg = jnp.where(causal, _dot_nt(k_own, qh), NEG_INF)
        m0 = jnp.max(lg, axis=0, keepdims=True)
        p0 = jnp.exp(lg - m0)
        l0 = jnp.sum(p0, axis=0, keepdims=True)
        acc0 = _dot(vt_ref[j, hh * head_dim:(hh + 1) * head_dim, :], p0.astype(BF16))
        carry += [m0, l0, acc0]

    def past_block(n, carry):
        kb = kbf_ref[pl.ds(pl.multiple_of(n * blk, blk), blk), :]
        out = []
        for hh in range(n_heads):
            m_run, l_run, acc = carry[3 * hh:3 * hh + 3]
            picked = sel_ref[hh, pl.ds(n, 1), :] > 0.5
            lg = jnp.where(picked, _dot_nt(kb, qhs[hh]), NEG_INF)
            m_new = jnp.maximum(m_run, jnp.max(lg, axis=0, keepdims=True))
            alpha = jnp.exp(m_run - m_new)
            p = jnp.exp(lg - m_new)
            l_new = alpha * l_run + jnp.sum(p, axis=0, keepdims=True)
            pv = _dot(vt_ref[n, hh * head_dim:(hh + 1) * head_dim, :], p.astype(BF16))
            out += [m_new, l_new, alpha * acc + pv]
        return tuple(out)

    carry = lax.fori_loop(0, j, past_block, tuple(carry))
    out_t = jnp.concatenate([carry[3 * hh + 2] / carry[3 * hh + 1] for hh in range(n_heads)], axis=0)
    o_ref[...] = out_t.T


def _moba_prompt_call(q, k, v, batch, seq, head_dim):
    n, width = q.shape
    pairs = width // LANES
    nb = seq // MOBA_BLOCK
    assert nb <= SUBLANES
    qspec = pl.BlockSpec((MOBA_BLOCK, LANES), lambda b, hp, j: (b * nb + j, hp))
    kvspec = pl.BlockSpec((seq, LANES), lambda b, hp, j: (b, hp))
    return pl.pallas_call(
        functools.partial(_moba_prompt_body, head_dim),
        out_shape=jax.ShapeDtypeStruct((n, width), F32),
        grid=(batch, pairs, nb),
        in_specs=[qspec, kvspec, kvspec],
        out_specs=qspec,
        scratch_shapes=[pltpu.VMEM((seq, LANES), BF16),
                        pltpu.VMEM((nb, LANES, MOBA_BLOCK), BF16),
                        pltpu.VMEM((nb, LANES), F32),
                        pltpu.VMEM((LANES // head_dim, nb, MOBA_BLOCK), F32)],
        compiler_params=_params(("parallel", "parallel", "arbitrary")),
        name="moba_prompt",
    )(q, k, v)


def _moba_sample_body(layer, n_new, pt_ref, q_ref, kn_ref, vn_ref, k_hbm, v_hbm, o_ref,
                      buf_ref, sem, qm_ref, logit_ref, ksum_ref, p_ref, acc_ref):
    s = pl.program_id(0)
    n_seq = pl.num_programs(0)
    _, cp, page, n_heads, hd = buf_ref.shape
    n_chunks = logit_ref.shape[0]
    rows = n_new * n_heads
    pages_per_block = MOBA_BLOCK // page
    blocks_per_chunk = cp // pages_per_block
    n_blocks = n_chunks * blocks_per_chunk

    def page_copy(pool, page_id, slot, i):
        return pltpu.make_async_copy(pool.at[layer, page_id], buf_ref.at[slot, i], sem.at[slot])

    def start_chunk(pool, seq, chunk, slot):
        for i in range(cp):
            page_copy(pool, pt_ref[seq, chunk * cp + i], slot, i).start()

    def wait_chunk(slot):
        for i in range(cp):
            page_copy(k_hbm, 0, slot, i).wait()

    @pl.when(s == 0)
    def _():
        start_chunk(k_hbm, 0, 0, 0)

    qf = q_ref[0].reshape(rows, hd) * (hd ** -0.5)
    row_head = lax.broadcasted_iota(jnp.int32, (rows, hd), 0) & (n_heads - 1)
    for h in range(n_heads):
        qm_ref[h] = jnp.where(row_head == h, qf, 0.0).astype(BF16)
    acc_ref[...] = jnp.zeros(acc_ref.shape, F32)

    def key_chunk(c, carry):
        slot = c & 1
        wait_chunk(slot)

        @pl.when(c + 1 < n_chunks)
        def _():
            start_chunk(k_hbm, s, c + 1, 1 - slot)

        @pl.when(c + 1 == n_chunks)
        def _():
            start_chunk(v_hbm, s, 0, 1 - slot)

        for i in range(cp):
            tot = jnp.sum(buf_ref[slot, i], axis=0)
            run = tot if i % pages_per_block == 0 else run + tot
            if i % pages_per_block == pages_per_block - 1:
                ksum_ref[c * blocks_per_chunk + i // pages_per_block] = run
            lg = None
            for h in range(n_heads):
                d = _dot_nt(qm_ref[h], buf_ref[slot, i, :, h, :].astype(BF16))
                lg = d if lg is None else lg + d
            logit_ref[c, :, i * page:(i + 1) * page] = lg
        return carry

    lax.fori_loop(0, n_chunks, key_chunk, 0)

    pad_rows = lambda a: jnp.concatenate([a, jnp.zeros((LANES - a.shape[0], hd), F32)], axis=0)
    gate, lg_new = None, None
    for h in range(n_heads):
        kmean = pad_rows(ksum_ref[:, h, :] * (1.0 / MOBA_BLOCK))
        g = _dot_nt(qm_ref[h], kmean.astype(BF16))
        d = _dot_nt(qm_ref[h], pad_rows(kn_ref[0, :, h, :]).astype(BF16))
        gate = g if gate is None else gate + g
        lg_new = d if lg_new is None else lg_new + d
    col = lax.broadcasted_iota(jnp.int32, (rows, LANES), 1)
    row = lax.broadcasted_iota(jnp.int32, (rows, LANES), 0)
    sel = _top_blocks(gate, col < n_blocks, n_blocks)
    self_f = jnp.where(sel, 1.0, 0.0)
    lg_new = jnp.where((col * n_heads <= row) & (col < n_new), lg_new, NEG_INF)
    m = jnp.max(lg_new, axis=-1, keepdims=True)
    masks = []
    for b in range(n_blocks):
        picked = self_f[:, b:b + 1] > 0.5
        masks.append(picked)
        c, o = divmod(b, blocks_per_chunk)
        lg = logit_ref[c, :, o * MOBA_BLOCK:(o + 1) * MOBA_BLOCK]
        m = jnp.maximum(m, jnp.max(jnp.where(picked, lg, NEG_INF), axis=-1, keepdims=True))
    p_new = jnp.exp(lg_new - m)
    l = jnp.sum(p_new, axis=-1, keepdims=True)
    for b in range(n_blocks):
        c, o = divmod(b, blocks_per_chunk)
        lg = logit_ref[c, :, o * MOBA_BLOCK:(o + 1) * MOBA_BLOCK]
        p = jnp.exp(jnp.where(masks[b], lg, NEG_INF) - m)
        l = l + jnp.sum(p, axis=-1, keepdims=True)
        p_ref[c, :, o * MOBA_BLOCK:(o + 1) * MOBA_BLOCK] = p.astype(BF16)

    def value_chunk(c, carry):
        slot = c & 1
        wait_chunk(slot)

        @pl.when(c + 1 < n_chunks)
        def _():
            start_chunk(v_hbm, s, c + 1, 1 - slot)

        @pl.when((c + 1 == n_chunks) & (s + 1 < n_seq))
        def _():
            start_chunk(k_hbm, s + 1, 0, 1 - slot)

        for h in range(n_heads):
            pv = None
            for i in range(cp):
                d = _dot(p_ref[c, :, i * page:(i + 1) * page], buf_ref[slot, i, :, h, :].astype(BF16))
                pv = d if pv is None else pv + d
            acc_ref[h] = acc_ref[h] + pv
        return carry

    lax.fori_loop(0, n_chunks, value_chunk, 0)

    linv = 1.0 / l
    p_new16 = p_new.astype(BF16)
    sub = lax.broadcasted_iota(jnp.int32, (n_heads, hd), 0)
    out = [jnp.zeros((n_heads, hd), F32) for _ in range(n_new)]
    for h in range(n_heads):
        acc = (acc_ref[h] + _dot(p_new16, pad_rows(vn_ref[0, :, h, :]).astype(BF16))) * linv
        for t in range(n_new):
            out[t] = jnp.where(sub == h, acc[t * n_heads:(t + 1) * n_heads, :], out[t])
    for t in range(n_new):
        o_ref[0, t] = out[t]


def _moba_sample_call(q, kn, vn, pool_k, pool_v, layer, page_table):
    seqs, n_new, n_heads, hd = q.shape
    page = pool_k.shape[2]
    n_pages = page_table.shape[1]
    cp = PAGES_PER_CHUNK
    n_chunks = n_pages // cp
    assert n_pages % cp == 0 and n_chunks % 2 == 0
    assert MOBA_BLOCK % page == 0 and cp % (MOBA_BLOCK // page) == 0
    assert n_pages * page // MOBA_BLOCK <= LANES and n_new <= SAMPLE_SLOT and n_heads == SUBLANES
    rows = n_new * n_heads
    span = cp * page

    per_seq = lambda a: pl.BlockSpec((1,) + a.shape[1:], lambda s, pt: (s, 0, 0, 0))
    grid_spec = pltpu.PrefetchScalarGridSpec(
        num_scalar_prefetch=1,
        grid=(seqs,),
        in_specs=[per_seq(q), per_seq(kn), per_seq(vn),
                  pl.BlockSpec(memory_space=pl.ANY), pl.BlockSpec(memory_space=pl.ANY)],
        out_specs=per_seq(q),
        scratch_shapes=[pltpu.VMEM((2, cp, page, n_heads, hd), F32),
                        pltpu.SemaphoreType.DMA((2,)),
                        pltpu.VMEM((n_heads, rows, hd), BF16),
                        pltpu.VMEM((n_chunks, rows, span), F32),
                        pltpu.VMEM((n_pages * page // MOBA_BLOCK, n_heads, hd), F32),
                        pltpu.VMEM((n_chunks, rows, span), BF16),
                        pltpu.VMEM((n_heads, rows, hd), F32)],
    )
    return pl.pallas_call(
        functools.partial(_moba_sample_body, layer, n_new),
        out_shape=jax.ShapeDtypeStruct(q.shape, F32),
        grid_spec=grid_spec,
        compiler_params=_params(("arbitrary",)),
        name="moba_sample",
    )(page_table, q, kn, vn, pool_k, pool_v)


def _unit_lower_inverse(l_strict, chunk):
    n = l_strict.shape[0]
    eye = jnp.where(lax.broadcasted_iota(jnp.int32, (n, n), 0) == lax.broadcasted_iota(jnp.int32, (n, n), 1),
                    1.0, 0.0)
    m = -l_strict
    s = eye + m
    m16 = m.astype(BF16)
    p = _dot(m16, m16)
    steps = chunk.bit_length() - 2
    for it in range(steps):
        p16 = p.astype(BF16)
        s = s + _dot(p16, s.astype(BF16))
        if it + 1 < steps:
            p = _dot(p16, p16)
    return s


def _gdn_prepare(xs_ref, base, rows, convw_ref, ba_ref, bat_ref, prow_ref, pcol_ref, chunk, n_heads, hd, valid):
    w = convw_ref[...]
    taps = w.shape[0]
    y = xs_ref[pl.ds(base, rows), :] * w[0:1, :]
    for t in range(1, taps):
        y = y + xs_ref[pl.ds(base + t, rows), :] * w[t:t + 1, :]
    y = _silu(y)
    bw = n_heads * hd
    ba = ba_ref[...]
    bat = bat_ref[...]
    beta_all = jax.nn.sigmoid(ba)
    g_all = -jnp.exp(prow_ref[0:1, :]) * _softplus(ba + prow_ref[1:2, :])
    g_rows = -jnp.exp(pcol_ref[:, 0:1]) * _softplus(bat + pcol_ref[:, 1:2])
    if valid is not None:
        vcol, vrow = valid
        beta_all = jnp.where(vcol, beta_all, 0.0)
        g_all = jnp.where(vcol, g_all, 0.0)
        g_rows = jnp.where(vrow, g_rows, 0.0)
    ri = lax.broadcasted_iota(jnp.int32, (rows, rows), 0)
    ci = lax.broadcasted_iota(jnp.int32, (rows, rows), 1)
    shift = chunk.bit_length() - 1
    same = (ri >> shift) == (ci >> shift)
    lower = jnp.where(same & (ci <= ri), 1.0, 0.0)
    upper = jnp.where(same & (ri <= ci), 1.0, 0.0)
    hi = lax.Precision.HIGHEST
    gc_cols = _dot(lower, g_all, hi)
    gc_rows = _dot(g_rows, upper, hi)
    heads = []
    for h in range(n_heads):
        q = y[:, h * hd:(h + 1) * hd]
        k = y[:, bw + h * hd:bw + (h + 1) * hd]
        v = y[:, 2 * bw + h * hd:2 * bw + (h + 1) * hd]
        q = q * lax.rsqrt(jnp.sum(q * q, axis=-1, keepdims=True) + L2_EPS) * (hd ** -0.5)
        k = k * lax.rsqrt(jnp.sum(k * k, axis=-1, keepdims=True) + L2_EPS)
        if valid is not None:
            k = jnp.where(valid[0], k, 0.0)
            v = jnp.where(valid[0], v, 0.0)
        beta = beta_all[:, h:h + 1]
        gcc = gc_cols[:, n_heads + h:n_heads + h + 1]
        gcr = gc_rows[n_heads + h:n_heads + h + 1, :]
        heads.append((q, k, v, beta, gcc, gcr))
    return heads, same, ri, ci


def _gdn_head_matrices(q, k, v, beta, gcc, gcr, same, ri, ci, chunk):
    causal = same & (ci <= ri)
    strict = same & (ci < ri)
    decay = jnp.where(causal, jnp.exp(jnp.where(causal, gcc - gcr, 0.0)), 0.0)
    kb = k * beta
    kb16 = kb.astype(BF16)
    k16 = k.astype(BF16)
    lmat = jnp.where(strict, _dot_nt(kb16, k16) * decay, 0.0)
    attn = _dot_nt(q.astype(BF16), k16) * decay
    tinv = _unit_lower_inverse(lmat, chunk)
    egc = jnp.exp(gcc)
    rhs = jnp.concatenate([v * beta, kb * egc], axis=1).astype(BF16)
    uw = _dot(tinv.astype(BF16), rhs)
    hd = q.shape[1]
    return uw[:, :hd], uw[:, hd:], attn.astype(BF16), q * egc


def _gated_out(o, gnorm_row, z):
    return _rms(o, gnorm_row) * _silu(z)


def _gdn_prompt_body(n_heads, hd, qkv_ref, z_ref, ba_ref, bat_ref, convw_ref, prow_ref, pcol_ref, gnorm_ref,
                     o_ref, sfin_ref, xs_ref, s_ref, vn_ref):
    i = pl.program_id(1)
    rows = qkv_ref.shape[0]
    chunk = GDN_CHUNK

    @pl.when(i == 0)
    def _():
        xs_ref[0:SUBLANES, :] = jnp.zeros((SUBLANES, xs_ref.shape[1]), F32)
        s_ref[...] = jnp.zeros(s_ref.shape, F32)

    x = qkv_ref[...]
    xs_ref[SUBLANES:SUBLANES + rows, :] = x
    taps = convw_ref.shape[0]
    heads, same, ri, ci = _gdn_prepare(xs_ref, SUBLANES - taps + 1, rows, convw_ref, ba_ref, bat_ref, prow_ref,
                                       pcol_ref, chunk, n_heads, hd, None)
    xs_ref[0:SUBLANES, :] = x[rows - SUBLANES:rows, :]

    gnorm = gnorm_ref[...]
    for h, (q, k, v, beta, gcc, gcr) in enumerate(heads):
        u, w, attn16, qg = _gdn_head_matrices(q, k, v, beta, gcc, gcr, same, ri, ci, chunk)
        vn_ref[...] = jnp.zeros(vn_ref.shape, BF16)
        s = s_ref[h]
        for c in range(rows // chunk):
            r0 = c * chunk
            s16 = s.astype(BF16)
            v_new = u[r0:r0 + chunk] - _dot(w[r0:r0 + chunk].astype(BF16), s16)
            vn_ref[r0:r0 + chunk, :] = v_new.astype(BF16)
            o = _dot(qg[r0:r0 + chunk].astype(BF16), s16) + _dot(attn16[r0:r0 + chunk, :], vn_ref[...])
            gl = gcc[r0 + chunk - 1:r0 + chunk, :]
            kd = (k[r0:r0 + chunk] * jnp.exp(gl - gcc[r0:r0 + chunk])).astype(BF16)
            s = s * jnp.exp(gl) + _dot_tn(kd, v_new.astype(BF16))
            zc = z_ref[r0:r0 + chunk, h * hd:(h + 1) * hd]
            o_ref[r0:r0 + chunk, h * hd:(h + 1) * hd] = _gated_out(o, gnorm, zc)
        s_ref[h] = s

    @pl.when(i == pl.num_programs(1) - 1)
    def _():
        sfin_ref[0] = s_ref[...]


def _gdn_prompt_call(qkv, z, ba, bat, convw, prow, pcol, gnorm, batch, seq, n_heads, hd):
    n, cdim = qkv.shape
    rows = GDN_ROWS
    nblk = seq // rows
    bw = n_heads * hd
    tile = lambda width: pl.BlockSpec((rows, width), lambda b, i: (b * nblk + i, 0))
    return pl.pallas_call(
        functools.partial(_gdn_prompt_body, n_heads, hd),
        out_shape=[jax.ShapeDtypeStruct((n, bw), F32), jax.ShapeDtypeStruct((batch, n_heads, hd, hd), F32)],
        grid=(batch, nblk),
        in_specs=[tile(cdim), tile(bw), tile(LANES),
                  pl.BlockSpec((SUBLANES, rows), lambda b, i: (0, b * nblk + i)),
                  _const_spec(convw.shape), _const_spec(prow.shape), _const_spec(pcol.shape),
                  _const_spec(gnorm.shape)],
        out_specs=[tile(bw), pl.BlockSpec((1, n_heads, hd, hd), lambda b, i: (b, 0, 0, 0))],
        scratch_shapes=[pltpu.VMEM((SUBLANES + rows, cdim), F32), pltpu.VMEM((n_heads, hd, hd), F32),
                        pltpu.VMEM((rows, hd), BF16)],
        compiler_params=_params(("parallel", "arbitrary")),
        name="gdn_prompt",
    )(qkv, z, ba, bat, convw, prow, pcol, gnorm)


def _gdn_sample_body(n_heads, hd, n_new, xp_ref, z_ref, ba_ref, bat_ref, s0_ref, convw_ref, prow_ref, pcol_ref,
                     gnorm_ref, o_ref, snew_ref, xs_ref):
    rows = xp_ref.shape[0]
    slot = SAMPLE_SLOT
    xs_ref[0:rows, :] = xp_ref[...]
    xs_ref[rows:rows + SUBLANES, :] = jnp.zeros((SUBLANES, xs_ref.shape[1]), F32)
    vcol = (lax.broadcasted_iota(jnp.int32, (rows, 1), 0) & (slot - 1)) < n_new
    vrow = (lax.broadcasted_iota(jnp.int32, (1, rows), 1) & (slot - 1)) < n_new
    heads, same, ri, ci = _gdn_prepare(xs_ref, 0, rows, convw_ref, ba_ref, bat_ref, prow_ref, pcol_ref,
                                       slot, n_heads, hd, (vcol, vrow))
    gnorm = gnorm_ref[...]
    for h, (q, k, v, beta, gcc, gcr) in enumerate(heads):
        u, w, attn16, qg = _gdn_head_matrices(q, k, v, beta, gcc, gcr, same, ri, ci, slot)
        v_parts, qs_parts = [], []
        for g in range(rows // slot):
            r0 = g * slot
            s16 = s0_ref[g, h].astype(BF16)
            lhs = jnp.concatenate([w[r0:r0 + slot], qg[r0:r0 + slot]], axis=0).astype(BF16)
            both = _dot(lhs, s16)
            v_parts.append(u[r0:r0 + slot] - both[:slot])
            qs_parts.append(both[slot:])
        v_new = jnp.concatenate(v_parts, axis=0)
        o = jnp.concatenate(qs_parts, axis=0) + _dot(attn16, v_new.astype(BF16))
        o_ref[:, h * hd:(h + 1) * hd] = _gated_out(o, gnorm, z_ref[:, h * hd:(h + 1) * hd])
        for g in range(rows // slot):
            r0 = g * slot
            gl = gcc[r0 + slot - 1:r0 + slot, :]
            kd = k[r0:r0 + slot] * jnp.exp(gl - gcc[r0:r0 + slot])
            snew_ref[g, h] = s0_ref[g, h] * jnp.exp(gl) + _dot_tn(kd, v_new[r0:r0 + slot])


def _gdn_sample_call(xp, z, ba, bat, s0, convw, prow, pcol, gnorm, n_heads, hd, n_new):
    n, cdim = xp.shape
    rows = SAMPLE_GROUP * SAMPLE_SLOT
    bw = n_heads * hd
    tile = lambda width: pl.BlockSpec((rows, width), lambda i: (i, 0))
    sspec = pl.BlockSpec((SAMPLE_GROUP, n_heads, hd, hd), lambda i: (i, 0, 0, 0))
    return pl.pallas_call(
        functools.partial(_gdn_sample_body, n_heads, hd, n_new),
        out_shape=[jax.ShapeDtypeStruct((n, bw), F32), jax.ShapeDtypeStruct(s0.shape, F32)],
        grid=(n // rows,),
        in_specs=[tile(cdim), tile(bw), tile(LANES), pl.BlockSpec((SUBLANES, rows), lambda i: (0, i)), sspec,
                  _const_spec(convw.shape), _const_spec(prow.shape), _const_spec(pcol.shape),
                  _const_spec(gnorm.shape)],
        out_specs=[tile(bw), sspec],
        scratch_shapes=[pltpu.VMEM((rows + SUBLANES, cdim), F32)],
        compiler_params=_params(("parallel",)),
        name="gdn_sample",
    )(xp, z, ba, bat, s0, convw, prow, pcol, gnorm)


def _row(v):
    return v.reshape(1, -1).astype(F32)


def kernel(x_prompt, x_sample, cache_k, cache_v, state_gdn, state_conv, page_table, norm_ffn1_pre, norm_ffn1_post, w_ffn1_in, w_ffn1_out, norm_mix_pre, norm_mix_post, w_mix_in, w_mix_out, gdn_conv_w, gdn_a_log, gdn_dt_bias, gdn_norm_w, norm_ffn2_pre, norm_ffn2_post, w_ffn2_in, w_ffn2_out):
    batch, seq, d_model = x_prompt.shape
    dec_batch, dec_seq, _ = x_sample.shape
    depth = w_ffn1_in.shape[0]
    d_ff = w_ffn1_out.shape[1]
    a_heads, a_hd = cache_k.shape[3], cache_k.shape[4]
    b_heads, b_hd = state_gdn.shape[2], state_gdn.shape[3]
    aw, bw = a_heads * a_hd, b_heads * b_hd
    conv_dim = gdn_conv_w.shape[2]
    taps = gdn_conv_w.shape[1]
    widths = (aw, aw, aw, conv_dim, bw)
    main = sum(widths)
    assert w_mix_in.shape[2] == main + 2 * b_heads and 2 * b_heads <= SUBLANES
    assert seq % GDN_ROWS == 0 and seq % MOBA_BLOCK == 0 and dec_batch % SAMPLE_GROUP == 0
    assert taps - 1 + dec_seq <= SAMPLE_SLOT and dec_seq >= taps - 1 and seq >= taps - 1
    pad = SAMPLE_SLOT - (taps - 1) - dec_seq

    yp = x_prompt.reshape(batch * seq, d_model)
    ys = x_sample.reshape(dec_batch * dec_seq, d_model)
    outs = [[] for _ in range(8)]
    for l in range(depth):
        wg1, wu1 = w_ffn1_in[l, :, :d_ff].astype(BF16), w_ffn1_in[l, :, d_ff:].astype(BF16)
        wo1 = w_ffn1_out[l].astype(BF16)
        wg2, wu2 = w_ffn2_in[l, :, :d_ff].astype(BF16), w_ffn2_in[l, :, d_ff:].astype(BF16)
        wo2 = w_ffn2_out[l].astype(BF16)
        w_main = w_mix_in[l, :, :main].astype(BF16)
        w_tail = w_mix_in[l, :, main:]
        w_ba = jnp.pad(w_tail, ((0, 0), (0, LANES - 2 * b_heads))).astype(BF16)
        w_bat = jnp.pad(w_tail.T, ((0, SUBLANES - 2 * b_heads), (0, 0))).astype(BF16)
        wma, wmb = w_mix_out[l, :aw].astype(BF16), w_mix_out[l, aw:].astype(BF16)
        a_log, dt_bias = gdn_a_log[l].astype(F32), gdn_dt_bias[l].astype(F32)
        prow = jnp.zeros((SUBLANES, LANES), F32)
        prow = prow.at[0, b_heads:2 * b_heads].set(a_log).at[1, b_heads:2 * b_heads].set(dt_bias)
        pcol = jnp.zeros((SUBLANES, LANES), F32)
        pcol = pcol.at[b_heads:2 * b_heads, 0].set(a_log).at[b_heads:2 * b_heads, 1].set(dt_bias)
        convw = gdn_conv_w[l].astype(F32)
        gnorm = _row(gdn_norm_w[l])

        def trunk_in(x):
            h = _ffn_call(x, _row(norm_ffn1_pre[l]), wg1, wu1, wo1, _row(norm_ffn1_post[l]))
            return (h,) + tuple(_proj_call(h, _row(norm_mix_pre[l]), w_main, w_ba, w_bat, widths))

        def trunk_out(h, oa, ob):
            return _mix_ffn_call(h, oa, ob, wma, wmb, _row(norm_mix_post[l]), _row(norm_ffn2_pre[l]),
                                 wg2, wu2, wo2, _row(norm_ffn2_post[l]))

        hp, qa, ka, va, qkv, z, ba, bat = trunk_in(yp)
        oa = _moba_prompt_call(qa, ka, va, batch, seq, a_hd)
        ob, s_p = _gdn_prompt_call(qkv, z, ba, bat, convw, prow, pcol, gnorm, batch, seq, b_heads, b_hd)
        yp = trunk_out(hp, oa, ob)
        outs[0].append(ka.reshape(batch, seq, a_heads, a_hd))
        outs[1].append(va.reshape(batch, seq, a_heads, a_hd))
        outs[4].append(s_p)
        outs[6].append(qkv.reshape(batch, seq, conv_dim)[:, seq - (taps - 1):, :])

        hs, qa, ka, va, qkv, z, ba, bat = trunk_in(ys)
        seq3 = lambda a: a.reshape(dec_batch, dec_seq, a.shape[-1])
        slots = lambda a: jnp.pad(seq3(a), ((0, 0), (0, SAMPLE_SLOT - dec_seq), (0, 0)))
        heads4 = lambda a: a.reshape(a.shape[0], a.shape[1], a_heads, a_hd)
        oa = _moba_sample_call(heads4(seq3(qa)), heads4(slots(ka)), heads4(slots(va)), cache_k, cache_v, l,
                               page_table)
        qkv3 = seq3(qkv)
        slot_rows = lambda a: slots(a).reshape(dec_batch * SAMPLE_SLOT, a.shape[-1])
        xp = jnp.concatenate([state_conv[l].astype(F32), qkv3, jnp.zeros((dec_batch, pad, conv_dim), F32)],
                             axis=1).reshape(dec_batch * SAMPLE_SLOT, conv_dim)
        bat_slots = jnp.pad(bat.reshape(SUBLANES, dec_batch, dec_seq),
                            ((0, 0), (0, 0), (0, SAMPLE_SLOT - dec_seq))).reshape(SUBLANES, dec_batch * SAMPLE_SLOT)
        ob_slots, s_s = _gdn_sample_call(xp, slot_rows(z), slot_rows(ba), bat_slots, state_gdn[l].astype(F32),
                                         convw, prow, pcol, gnorm, b_heads, b_hd, dec_seq)
        ob = ob_slots.reshape(dec_batch, SAMPLE_SLOT, bw)[:, :dec_seq].reshape(dec_batch * dec_seq, bw)
        ys = trunk_out(hs, oa.reshape(dec_batch * dec_seq, aw), ob)
        outs[2].append(ka.reshape(dec_batch, dec_seq, a_heads, a_hd))
        outs[3].append(va.reshape(dec_batch, dec_seq, a_heads, a_hd))
        outs[5].append(s_s)
        outs[7].append(jnp.concatenate([state_conv[l].astype(F32), qkv3], axis=1)[:, dec_seq:, :])

    stacked = [jnp.stack(o) for o in outs]
    return (yp.reshape(batch, seq, d_model), ys.reshape(dec_batch, dec_seq, d_model), *stacked)
```

```python
import functools

import jax
import jax.numpy as jnp
from jax import lax
from jax.experimental import pallas as pl
from jax.experimental.pallas import tpu as pltpu

F32 = jnp.float32
BF16 = jnp.bfloat16

MOBA_BLOCK = 256
MOBA_TOPK = 3
GDN_CHUNK = 64
RMS_EPS = 1e-6
L2_EPS = 1e-6
NEG_INF = -1e30

LANES = 128
SUBLANES = 8
VMEM_LIMIT = 56 * 1024 * 1024

TOKEN_TILE = 512
FF_CHUNK = 256
GDN_ROWS = 256
SAMPLE_SLOT = 8
SAMPLE_GROUP = 16
PAGES_PER_CHUNK = 8


def _dot(a, b, precision=None):
    return jnp.dot(a, b, preferred_element_type=F32, precision=precision)


def _dot_nt(a, b, precision=None):
    return lax.dot_general(a, b, (((1,), (1,)), ((), ())), preferred_element_type=F32, precision=precision)


def _dot_tn(a, b, precision=None):
    return lax.dot_general(a, b, (((0,), (0,)), ((), ())), preferred_element_type=F32, precision=precision)


def _rms(x, w_row):
    ms = jnp.mean(x * x, axis=-1, keepdims=True)
    return x * lax.rsqrt(ms + RMS_EPS) * w_row


def _silu(x):
    return x * jax.nn.sigmoid(x)


def _softplus(x):
    return jnp.maximum(x, 0.0) + jnp.log1p(jnp.exp(-jnp.abs(x)))


def _const_spec(shape):
    return pl.BlockSpec(shape, lambda *_: (0,) * len(shape), pipeline_mode=pl.Buffered(1))


def _params(semantics):
    return pltpu.CompilerParams(dimension_semantics=semantics, vmem_limit_bytes=VMEM_LIMIT)


def _swiglu_residual(x, npre, wg_ref, wu_ref, wo_ref, npost):
    xn = _rms(x, npre).astype(BF16)
    d_ff = wg_ref.shape[1]
    acc = jnp.zeros(x.shape, F32)
    for c0 in range(0, d_ff, FF_CHUNK):
        gate = _dot(xn, wg_ref[:, c0:c0 + FF_CHUNK])
        up = _dot(xn, wu_ref[:, c0:c0 + FF_CHUNK])
        act = (_silu(gate) * up).astype(BF16)
        acc = acc + _dot(act, wo_ref[c0:c0 + FF_CHUNK, :])
    return x + 0.5 * _rms(acc, npost)


def _ffn_body(x_ref, npre_ref, wg_ref, wu_ref, wo_ref, npost_ref, o_ref):
    o_ref[...] = _swiglu_residual(x_ref[...], npre_ref[...], wg_ref, wu_ref, wo_ref, npost_ref[...])


def _mix_ffn_body(h_ref, oa_ref, ob_ref, wma_ref, wmb_ref, nmix_ref,
                  npre_ref, wg_ref, wu_ref, wo_ref, npost_ref, o_ref):
    mix = _dot(oa_ref[...].astype(BF16), wma_ref[...]) + _dot(ob_ref[...].astype(BF16), wmb_ref[...])
    h2 = h_ref[...] + _rms(mix, nmix_ref[...])
    o_ref[...] = _swiglu_residual(h2, npre_ref[...], wg_ref, wu_ref, wo_ref, npost_ref[...])


def _token_tile(n):
    return TOKEN_TILE if n % TOKEN_TILE == 0 else n


def _ffn_call(x, npre, wg, wu, wo, npost):
    n, d = x.shape
    tm = _token_tile(n)
    row = pl.BlockSpec((tm, d), lambda i: (i, 0))
    return pl.pallas_call(
        _ffn_body,
        out_shape=jax.ShapeDtypeStruct((n, d), F32),
        grid=(n // tm,),
        in_specs=[row, _const_spec(npre.shape), _const_spec(wg.shape), _const_spec(wu.shape),
                  _const_spec(wo.shape), _const_spec(npost.shape)],
        out_specs=row,
        compiler_params=_params(("parallel",)),
        name="ffn1",
    )(x, npre, wg, wu, wo, npost)


def _mix_ffn_call(h, oa, ob, wma, wmb, nmix, npre, wg, wu, wo, npost):
    n, d = h.shape
    tm = _token_tile(n)
    row = pl.BlockSpec((tm, d), lambda i: (i, 0))
    half = pl.BlockSpec((tm, oa.shape[1]), lambda i: (i, 0))
    consts = [wma, wmb, nmix, npre, wg, wu, wo, npost]
    return pl.pallas_call(
        _mix_ffn_body,
        out_shape=jax.ShapeDtypeStruct((n, d), F32),
        grid=(n // tm,),
        in_specs=[row, half, half] + [_const_spec(c.shape) for c in consts],
        out_specs=row,
        compiler_params=_params(("parallel",)),
        name="mix_ffn2",
    )(h, oa, ob, *consts)


def _proj_body(widths, h_ref, nrm_ref, w_ref, wba_ref, wbat_ref, *out_refs):
    a = _rms(h_ref[...], nrm_ref[...]).astype(BF16)
    c0 = 0
    for ref, width in zip(out_refs[:len(widths)], widths):
        ref[...] = _dot(a, w_ref[:, c0:c0 + width])
        c0 += width
    ba_ref, bat_ref = out_refs[len(widths):]
    ba_ref[...] = _dot(a, wba_ref[...])
    bat_ref[...] = _dot_nt(wbat_ref[...], a)


def _proj_call(h, nrm, w_main, w_ba, w_bat, widths):
    n, d = h.shape
    tm = _token_tile(n)
    outs = [jax.ShapeDtypeStruct((n, w), F32) for w in widths]
    outs += [jax.ShapeDtypeStruct((n, LANES), F32), jax.ShapeDtypeStruct((SUBLANES, n), F32)]
    out_specs = [pl.BlockSpec((tm, w), lambda i: (i, 0)) for w in widths]
    out_specs += [pl.BlockSpec((tm, LANES), lambda i: (i, 0)), pl.BlockSpec((SUBLANES, tm), lambda i: (0, i))]
    return pl.pallas_call(
        functools.partial(_proj_body, tuple(widths)),
        out_shape=outs,
        grid=(n // tm,),
        in_specs=[pl.BlockSpec((tm, d), lambda i: (i, 0)), _const_spec(nrm.shape), _const_spec(w_main.shape),
                  _const_spec(w_ba.shape), _const_spec(w_bat.shape)],
        out_specs=out_specs,
        compiler_params=_params(("parallel",)),
        name="mix_in_proj",
    )(h, nrm, w_main, w_ba, w_bat)


def _top_blocks_t(scores, valid):
    row = lax.broadcasted_iota(jnp.int32, scores.shape, 0)
    s = jnp.where(valid, scores, NEG_INF)
    rank = jnp.zeros(scores.shape, jnp.int32)
    for m in range(scores.shape[0]):
        sm = s[m:m + 1, :]
        ahead = (sm > s) | ((sm == s) & (row > m))
        rank = rank + jnp.where(ahead, 1, 0)
    return valid & (rank < MOBA_TOPK)


def _moba_prompt_body(head_dim, q_ref, k_ref, v_ref, o_ref, kbf_ref, vt_ref, kmean_ref, sel_ref):
    j = pl.program_id(2)
    n_blocks = vt_ref.shape[0]
    n_heads = LANES // head_dim
    blk = MOBA_BLOCK

    @pl.when(j == 0)
    def _():
        kbf_ref[...] = k_ref[...].astype(BF16)
        for n in range(n_blocks):
            rows = slice(n * blk, (n + 1) * blk)
            vt_ref[n] = v_ref[rows, :].T.astype(BF16)
            kmean_ref[n:n + 1, :] = jnp.mean(k_ref[rows, :], axis=0, keepdims=True)

    q = q_ref[...] * (head_dim ** -0.5)
    lane = lax.broadcasted_iota(jnp.int32, q.shape, 1)
    key_i = lax.broadcasted_iota(jnp.int32, (blk, blk), 0)
    qry_i = lax.broadcasted_iota(jnp.int32, (blk, blk), 1)
    causal = key_i <= qry_i
    cand = lax.broadcasted_iota(jnp.int32, (n_blocks, blk), 0)
    kmean = kmean_ref[...].astype(BF16)
    k_own = kbf_ref[pl.ds(pl.multiple_of(j * blk, blk), blk), :]

    qhs, carry = [], []
    for hh in range(n_heads):
        in_head = (lane >= hh * head_dim) & (lane < (hh + 1) * head_dim)
        qh = jnp.where(in_head, q, 0.0).astype(BF16)
        qhs.append(qh)
        sel = _top_blocks_t(_dot_nt(kmean, qh), cand < j)
        sel_ref[hh] = jnp.where(sel, 1.0, 0.0)
        lg = jnp.where(causal, _dot_nt(k_own, qh), NEG_INF)
        m0 = jnp.max(lg, axis=0, keepdims=True)
        p0 = jnp.exp(lg - m0)
        l0 = jnp.sum(p0, axis=0, keepdims=True)
        acc0 = _dot(vt_ref[j, hh * head_dim:(hh + 1) * head_dim, :], p0.astype(BF16))
        carry += [m0, l0, acc0]

    def past_block(n, carry):
        kb = kbf_ref[pl.ds(pl.multiple_of(n * blk, blk), blk), :]
        out = []
        for hh in range(n_heads):
            m_run, l_run, acc = carry[3 * hh:3 * hh + 3]
            picked = sel_ref[hh, pl.ds(n, 1), :] > 0.5
            lg = jnp.where(picked, _dot_nt(kb, qhs[hh]), NEG_INF)
            m_new = jnp.maximum(m_run, jnp.max(lg, axis=0, keepdims=True))
            alpha = jnp.exp(m_run - m_new)
            p = jnp.exp(lg - m_new)
            l_new = alpha * l_run + jnp.sum(p, axis=0, keepdims=True)
            pv = _dot(vt_ref[n, hh * head_dim:(hh + 1) * head_dim, :], p.astype(BF16))
            out += [m_new, l_new, alpha * acc + pv]
        return tuple(out)

    carry = lax.fori_loop(0, j, past_block, tuple(carry))
    out_t = jnp.concatenate([carry[3 * hh + 2] / carry[3 * hh + 1] for hh in range(n_heads)], axis=0)
    o_ref[...] = out_t.T


def _moba_prompt_call(q, k, v, batch, seq, head_dim):
    n, width = q.shape
    pairs = width // LANES
    nb = seq // MOBA_BLOCK
    assert nb <= SUBLANES
    qspec = pl.BlockSpec((MOBA_BLOCK, LANES), lambda b, hp, j: (b * nb + j, hp))
    kvspec = pl.BlockSpec((seq, LANES), lambda b, hp, j: (b, hp))
    return pl.pallas_call(
        functools.partial(_moba_prompt_body, head_dim),
        out_shape=jax.ShapeDtypeStruct((n, width), F32),
        grid=(batch, pairs, nb),
        in_specs=[qspec, kvspec, kvspec],
        out_specs=qspec,
        scratch_shapes=[pltpu.VMEM((seq, LANES), BF16),
                        pltpu.VMEM((nb, LANES, MOBA_BLOCK), BF16),
                        pltpu.VMEM((nb, LANES), F32),
                        pltpu.VMEM((LANES // head_dim, nb, MOBA_BLOCK), F32)],
        compiler_params=_params(("parallel", "parallel", "arbitrary")),
        name="moba_prompt",
    )(q, k, v)


def _moba_sample_body(layer, n_new, n_heads, pt_ref, q_ref, kn_ref, vn_ref, k_hbm, v_hbm, o_ref,
                      kbuf, vbuf, sem, ot_ref):
    s = pl.program_id(0)
    n_seq = pl.num_programs(0)
    _, cp, width, page = kbuf.shape
    hd = width // n_heads
    n_chunks = pt_ref.shape[1] // cp
    pages_per_block = MOBA_BLOCK // page
    blocks_per_chunk = cp // pages_per_block
    n_blocks = n_chunks * blocks_per_chunk
    rows = LANES

    def page_copies(seq, chunk, slot, i):
        page_id = pt_ref[seq, chunk * cp + i]
        return (pltpu.make_async_copy(k_hbm.at[layer, page_id], kbuf.at[slot, i], sem.at[slot]),
                pltpu.make_async_copy(v_hbm.at[layer, page_id], vbuf.at[slot, i], sem.at[slot]))

    def start_chunk(seq, chunk, slot):
        for i in range(cp):
            for copy in page_copies(seq, chunk, slot, i):
                copy.start()

    def wait_chunk(slot):
        for i in range(cp):
            for copy in page_copies(0, 0, slot, i):
                copy.wait()

    @pl.when(s == 0)
    def _():
        start_chunk(0, 0, 0)

    q = q_ref[0] * (hd ** -0.5)
    lane_w = lax.broadcasted_iota(jnp.int32, (n_heads, width), 1)
    head_w = lax.broadcasted_iota(jnp.int32, (n_heads, width), 0)
    in_head = (lane_w >= head_w * hd) & (lane_w < (head_w + 1) * hd)
    q_rows = [jnp.where(in_head, jnp.broadcast_to(q[t:t + 1, :], (n_heads, width)), 0.0) for t in range(n_new)]
    q_rows.append(jnp.zeros((rows - n_new * n_heads, width), F32))
    qm = jnp.concatenate(q_rows, axis=0).astype(BF16)
    col_head = lax.broadcasted_iota(jnp.int32, (hd, rows), 1) & (n_heads - 1)
    lane = lax.broadcasted_iota(jnp.int32, (rows, LANES), 1)

    def block_stats(kt, vt, mask):
        logits = _dot(qm, kt)
        if mask is not None:
            logits = jnp.where(mask, logits, NEG_INF)
        g = jnp.sum(logits, axis=-1, keepdims=True)
        m = jnp.max(logits, axis=-1, keepdims=True)
        p = jnp.exp(logits - m)
        l = jnp.sum(p, axis=-1, keepdims=True)
        ot_all = _dot_nt(vt, p.astype(BF16))
        ot = jnp.zeros((hd, rows), F32)
        for h in range(n_heads):
            ot = jnp.where(col_head == h, ot_all[h * hd:(h + 1) * hd, :], ot)
        return g, m, l, ot

    def put(stat, b, column):
        return jnp.where(lane == b, column, stat)

    def chunk_step(c, stats):
        slot = c & 1
        wait_chunk(slot)

        @pl.when(c + 1 < n_chunks)
        def _():
            start_chunk(s, c + 1, 1 - slot)

        @pl.when((c + 1 == n_chunks) & (s + 1 < n_seq))
        def _():
            start_chunk(s + 1, 0, 1 - slot)

        g_all, m_all, l_all = stats
        for j in range(blocks_per_chunk):
            pages = range(j * pages_per_block, (j + 1) * pages_per_block)
            kt = jnp.concatenate([kbuf[slot, i] for i in pages], axis=1).astype(BF16)
            vt = jnp.concatenate([vbuf[slot, i] for i in pages], axis=1).astype(BF16)
            g, m, l, ot = block_stats(kt, vt, None)
            b = c * blocks_per_chunk + j
            ot_ref[b] = ot
            g_all, m_all, l_all = put(g_all, b, g), put(m_all, b, m), put(l_all, b, l)
        return g_all, m_all, l_all

    init = (jnp.zeros((rows, LANES), F32),) * 3
    g_all, m_all, l_all = lax.fori_loop(0, n_chunks, chunk_step, init)

    pad_t = lambda a: jnp.concatenate([a, jnp.zeros((LANES - a.shape[0], width), F32)], axis=0).T.astype(BF16)
    row = lax.broadcasted_iota(jnp.int32, (rows, LANES), 0)
    own_ok = (lane * n_heads <= row) & (lane < n_new)
    _, m_own, l_own, ot_own = block_stats(pad_t(kn_ref[0]), pad_t(vn_ref[0]), own_ok)
    m_all, l_all = put(m_all, n_blocks, m_own), put(l_all, n_blocks, l_own)

    gt, mt, lt = g_all.T[:n_blocks], m_all.T, l_all.T
    sel = _top_blocks_t(gt, lax.broadcasted_iota(jnp.int32, gt.shape, 0) < n_blocks)
    m_own_t, l_own_t = mt[n_blocks:n_blocks + 1], lt[n_blocks:n_blocks + 1]
    mt, lt = mt[:n_blocks], lt[:n_blocks]
    m_fin = jnp.maximum(jnp.max(jnp.where(sel, mt, NEG_INF), axis=0, keepdims=True), m_own_t)
    w = jnp.where(sel, jnp.exp(mt - m_fin), 0.0)
    w_own = jnp.exp(m_own_t - m_fin)
    denom = jnp.sum(w * lt, axis=0, keepdims=True) + w_own * l_own_t
    out_t = w_own * ot_own
    for b in range(n_blocks):
        out_t = out_t + w[b:b + 1, :] * ot_ref[b]
    out = (out_t / denom).T
    o_ref[0] = out[:n_new * n_heads].reshape(n_new, n_heads, hd)


def _moba_sample_call(q, kn, vn, pool_kt, pool_vt, layer, page_table, n_heads):
    seqs, n_new, width = q.shape
    page = pool_kt.shape[3]
    hd = width // n_heads
    n_pages = page_table.shape[1]
    cp = PAGES_PER_CHUNK
    n_chunks = n_pages // cp
    n_blocks = n_pages * page // MOBA_BLOCK
    assert n_pages % cp == 0 and n_chunks % 2 == 0
    assert MOBA_BLOCK % page == 0 and cp % (MOBA_BLOCK // page) == 0 and page == LANES
    assert n_blocks < LANES and n_new * n_heads <= LANES and n_new <= SAMPLE_SLOT

    per_seq = lambda a: pl.BlockSpec((1,) + a.shape[1:], lambda s, pt: (s, 0, 0))
    grid_spec = pltpu.PrefetchScalarGridSpec(
        num_scalar_prefetch=1,
        grid=(seqs,),
        in_specs=[per_seq(q), per_seq(kn), per_seq(vn),
                  pl.BlockSpec(memory_space=pl.ANY), pl.BlockSpec(memory_space=pl.ANY)],
        out_specs=pl.BlockSpec((1, n_new, n_heads, hd), lambda s, pt: (s, 0, 0, 0)),
        scratch_shapes=[pltpu.VMEM((2, cp, width, page), F32),
                        pltpu.VMEM((2, cp, width, page), F32),
                        pltpu.SemaphoreType.DMA((2,)),
                        pltpu.VMEM((n_blocks, hd, LANES), F32)],
    )
    return pl.pallas_call(
        functools.partial(_moba_sample_body, layer, n_new, n_heads),
        out_shape=jax.ShapeDtypeStruct((seqs, n_new, n_heads, hd), F32),
        grid_spec=grid_spec,
        compiler_params=_params(("arbitrary",)),
        name="moba_sample",
    )(page_table, q, kn, vn, pool_kt, pool_vt)


def _unit_lower_inverse(l_strict, chunk):
    n = l_strict.shape[0]
    eye = jnp.where(lax.broadcasted_iota(jnp.int32, (n, n), 0) == lax.broadcasted_iota(jnp.int32, (n, n), 1),
                    1.0, 0.0)
    m = -l_strict
    s = eye + m
    m16 = m.astype(BF16)
    p = _dot(m16, m16)
    steps = chunk.bit_length() - 2
    for it in range(steps):
        p16 = p.astype(BF16)
        s = s + _dot(p16, s.astype(BF16))
        if it + 1 < steps:
            p = _dot(p16, p16)
    return s


def _gdn_prepare(xs_ref, base, rows, convw_ref, ba_ref, bat_ref, prow_ref, pcol_ref, chunk, n_heads, hd, valid):
    w = convw_ref[...]
    taps = w.shape[0]
    y = xs_ref[pl.ds(base, rows), :] * w[0:1, :]
    for t in range(1, taps):
        y = y + xs_ref[pl.ds(base + t, rows), :] * w[t:t + 1, :]
    y = _silu(y)
    bw = n_heads * hd
    ba = ba_ref[...]
    bat = bat_ref[...]
    beta_all = jax.nn.sigmoid(ba)
    g_all = -jnp.exp(prow_ref[0:1, :]) * _softplus(ba + prow_ref[1:2, :])
    g_rows = -jnp.exp(pcol_ref[:, 0:1]) * _softplus(bat + pcol_ref[:, 1:2])
    if valid is not None:
        vcol, vrow = valid
        beta_all = jnp.where(vcol, beta_all, 0.0)
        g_all = jnp.where(vcol, g_all, 0.0)
        g_rows = jnp.where(vrow, g_rows, 0.0)
    ri = lax.broadcasted_iota(jnp.int32, (rows, rows), 0)
    ci = lax.broadcasted_iota(jnp.int32, (rows, rows), 1)
    shift = chunk.bit_length() - 1
    same = (ri >> shift) == (ci >> shift)
    lower = jnp.where(same & (ci <= ri), 1.0, 0.0)
    upper = jnp.where(same & (ri <= ci), 1.0, 0.0)
    hi = lax.Precision.HIGHEST
    gc_cols = _dot(lower, g_all, hi)
    gc_rows = _dot(g_rows, upper, hi)
    heads = []
    for h in range(n_heads):
        q = y[:, h * hd:(h + 1) * hd]
        k = y[:, bw + h * hd:bw + (h + 1) * hd]
        v = y[:, 2 * bw + h * hd:2 * bw + (h + 1) * hd]
        q = q * lax.rsqrt(jnp.sum(q * q, axis=-1, keepdims=True) + L2_EPS) * (hd ** -0.5)
        k = k * lax.rsqrt(jnp.sum(k * k, axis=-1, keepdims=True) + L2_EPS)
        if valid is not None:
            k = jnp.where(valid[0], k, 0.0)
            v = jnp.where(valid[0], v, 0.0)
        beta = beta_all[:, h:h + 1]
        gcc = gc_cols[:, n_heads + h:n_heads + h + 1]
        gcr = gc_rows[n_heads + h:n_heads + h + 1, :]
        heads.append((q, k, v, beta, gcc, gcr))
    return heads, same, ri, ci


def _gdn_head_matrices(q, k, v, beta, gcc, gcr, same, ri, ci, chunk):
    causal = same & (ci <= ri)
    strict = same & (ci < ri)
    decay = jnp.where(causal, jnp.exp(jnp.where(causal, gcc - gcr, 0.0)), 0.0)
    kb = k * beta
    kb16 = kb.astype(BF16)
    k16 = k.astype(BF16)
    lmat = jnp.where(strict, _dot_nt(kb16, k16) * decay, 0.0)
    attn = _dot_nt(q.astype(BF16), k16) * decay
    tinv = _unit_lower_inverse(lmat, chunk)
    egc = jnp.exp(gcc)
    rhs = jnp.concatenate([v * beta, kb * egc], axis=1).astype(BF16)
    uw = _dot(tinv.astype(BF16), rhs)
    hd = q.shape[1]
    return uw[:, :hd], uw[:, hd:], attn.astype(BF16), q * egc


def _gated_out(o, gnorm_row, z):
    return _rms(o, gnorm_row) * _silu(z)


def _gdn_prompt_body(n_heads, hd, qkv_ref, z_ref, ba_ref, bat_ref, convw_ref, prow_ref, pcol_ref, gnorm_ref,
                     o_ref, sfin_ref, xs_ref, s_ref, vn_ref):
    i = pl.program_id(1)
    rows = qkv_ref.shape[0]
    chunk = GDN_CHUNK

    @pl.when(i == 0)
    def _():
        xs_ref[0:SUBLANES, :] = jnp.zeros((SUBLANES, xs_ref.shape[1]), F32)
        s_ref[...] = jnp.zeros(s_ref.shape, F32)

    x = qkv_ref[...]
    xs_ref[SUBLANES:SUBLANES + rows, :] = x
    taps = convw_ref.shape[0]
    heads, same, ri, ci = _gdn_prepare(xs_ref, SUBLANES - taps + 1, rows, convw_ref, ba_ref, bat_ref, prow_ref,
                                       pcol_ref, chunk, n_heads, hd, None)
    xs_ref[0:SUBLANES, :] = x[rows - SUBLANES:rows, :]

    gnorm = gnorm_ref[...]
    for h, (q, k, v, beta, gcc, gcr) in enumerate(heads):
        u, w, attn16, qg = _gdn_head_matrices(q, k, v, beta, gcc, gcr, same, ri, ci, chunk)
        vn_ref[...] = jnp.zeros(vn_ref.shape, BF16)
        s = s_ref[h]
        for c in range(rows // chunk):
            r0 = c * chunk
            s16 = s.astype(BF16)
            v_new = u[r0:r0 + chunk] - _dot(w[r0:r0 + chunk].astype(BF16), s16)
            vn_ref[r0:r0 + chunk, :] = v_new.astype(BF16)
            o = _dot(qg[r0:r0 + chunk].astype(BF16), s16) + _dot(attn16[r0:r0 + chunk, :], vn_ref[...])
            gl = gcc[r0 + chunk - 1:r0 + chunk, :]
            kd = (k[r0:r0 + chunk] * jnp.exp(gl - gcc[r0:r0 + chunk])).astype(BF16)
            s = s * jnp.exp(gl) + _dot_tn(kd, v_new.astype(BF16))
            zc = z_ref[r0:r0 + chunk, h * hd:(h + 1) * hd]
            o_ref[r0:r0 + chunk, h * hd:(h + 1) * hd] = _gated_out(o, gnorm, zc)
        s_ref[h] = s

    @pl.when(i == pl.num_programs(1) - 1)
    def _():
        sfin_ref[0] = s_ref[...]


def _gdn_prompt_call(qkv, z, ba, bat, convw, prow, pcol, gnorm, batch, seq, n_heads, hd):
    n, cdim = qkv.shape
    rows = GDN_ROWS
    nblk = seq // rows
    bw = n_heads * hd
    tile = lambda width: pl.BlockSpec((rows, width), lambda b, i: (b * nblk + i, 0))
    return pl.pallas_call(
        functools.partial(_gdn_prompt_body, n_heads, hd),
        out_shape=[jax.ShapeDtypeStruct((n, bw), F32), jax.ShapeDtypeStruct((batch, n_heads, hd, hd), F32)],
        grid=(batch, nblk),
        in_specs=[tile(cdim), tile(bw), tile(LANES),
                  pl.BlockSpec((SUBLANES, rows), lambda b, i: (0, b * nblk + i)),
                  _const_spec(convw.shape), _const_spec(prow.shape), _const_spec(pcol.shape),
                  _const_spec(gnorm.shape)],
        out_specs=[tile(bw), pl.BlockSpec((1, n_heads, hd, hd), lambda b, i: (b, 0, 0, 0))],
        scratch_shapes=[pltpu.VMEM((SUBLANES + rows, cdim), F32), pltpu.VMEM((n_heads, hd, hd), F32),
                        pltpu.VMEM((rows, hd), BF16)],
        compiler_params=_params(("parallel", "arbitrary")),
        name="gdn_prompt",
    )(qkv, z, ba, bat, convw, prow, pcol, gnorm)


def _gdn_sample_body(n_heads, hd, n_new, xp_ref, z_ref, ba_ref, bat_ref, s0_ref, convw_ref, prow_ref, pcol_ref,
                     gnorm_ref, o_ref, snew_ref, xs_ref):
    rows = xp_ref.shape[0]
    slot = SAMPLE_SLOT
    xs_ref[0:rows, :] = xp_ref[...]
    xs_ref[rows:rows + SUBLANES, :] = jnp.zeros((SUBLANES, xs_ref.shape[1]), F32)
    vcol = (lax.broadcasted_iota(jnp.int32, (rows, 1), 0) & (slot - 1)) < n_new
    vrow = (lax.broadcasted_iota(jnp.int32, (1, rows), 1) & (slot - 1)) < n_new
    heads, same, ri, ci = _gdn_prepare(xs_ref, 0, rows, convw_ref, ba_ref, bat_ref, prow_ref, pcol_ref,
                                       slot, n_heads, hd, (vcol, vrow))
    gnorm = gnorm_ref[...]
    for h, (q, k, v, beta, gcc, gcr) in enumerate(heads):
        u, w, attn16, qg = _gdn_head_matrices(q, k, v, beta, gcc, gcr, same, ri, ci, slot)
        v_parts, qs_parts = [], []
        for g in range(rows // slot):
            r0 = g * slot
            s16 = s0_ref[g, h].astype(BF16)
            lhs = jnp.concatenate([w[r0:r0 + slot], qg[r0:r0 + slot]], axis=0).astype(BF16)
            both = _dot(lhs, s16)
            v_parts.append(u[r0:r0 + slot] - both[:slot])
            qs_parts.append(both[slot:])
        v_new = jnp.concatenate(v_parts, axis=0)
        o = jnp.concatenate(qs_parts, axis=0) + _dot(attn16, v_new.astype(BF16))
        o_ref[:, h * hd:(h + 1) * hd] = _gated_out(o, gnorm, z_ref[:, h * hd:(h + 1) * hd])
        for g in range(rows // slot):
            r0 = g * slot
            gl = gcc[r0 + slot - 1:r0 + slot, :]
            kd = k[r0:r0 + slot] * jnp.exp(gl - gcc[r0:r0 + slot])
            snew_ref[g, h] = s0_ref[g, h] * jnp.exp(gl) + _dot_tn(kd, v_new[r0:r0 + slot])


def _gdn_sample_call(xp, z, ba, bat, s0, convw, prow, pcol, gnorm, n_heads, hd, n_new):
    n, cdim = xp.shape
    rows = SAMPLE_GROUP * SAMPLE_SLOT
    bw = n_heads * hd
    tile = lambda width: pl.BlockSpec((rows, width), lambda i: (i, 0))
    sspec = pl.BlockSpec((SAMPLE_GROUP, n_heads, hd, hd), lambda i: (i, 0, 0, 0))
    return pl.pallas_call(
        functools.partial(_gdn_sample_body, n_heads, hd, n_new),
        out_shape=[jax.ShapeDtypeStruct((n, bw), F32), jax.ShapeDtypeStruct(s0.shape, F32)],
        grid=(n // rows,),
        in_specs=[tile(cdim), tile(bw), tile(LANES), pl.BlockSpec((SUBLANES, rows), lambda i: (0, i)), sspec,
                  _const_spec(convw.shape), _const_spec(prow.shape), _const_spec(pcol.shape),
                  _const_spec(gnorm.shape)],
        out_specs=[tile(bw), sspec],
        scratch_shapes=[pltpu.VMEM((rows + SUBLANES, cdim), F32)],
        compiler_params=_params(("parallel",)),
        name="gdn_sample",
    )(xp, z, ba, bat, s0, convw, prow, pcol, gnorm)


def _row(v):
    return v.reshape(1, -1).astype(F32)


def kernel(x_prompt, x_sample, cache_k, cache_v, state_gdn, state_conv, page_table, norm_ffn1_pre, norm_ffn1_post, w_ffn1_in, w_ffn1_out, norm_mix_pre, norm_mix_post, w_mix_in, w_mix_out, gdn_conv_w, gdn_a_log, gdn_dt_bias, gdn_norm_w, norm_ffn2_pre, norm_ffn2_post, w_ffn2_in, w_ffn2_out):
    batch, seq, d_model = x_prompt.shape
    dec_batch, dec_seq, _ = x_sample.shape
    depth = w_ffn1_in.shape[0]
    d_ff = w_ffn1_out.shape[1]
    a_heads, a_hd = cache_k.shape[3], cache_k.shape[4]
    b_heads, b_hd = state_gdn.shape[2], state_gdn.shape[3]
    aw, bw = a_heads * a_hd, b_heads * b_hd
    conv_dim = gdn_conv_w.shape[2]
    taps = gdn_conv_w.shape[1]
    widths = (aw, aw, aw, conv_dim, bw)
    main = sum(widths)
    assert w_mix_in.shape[2] == main + 2 * b_heads and 2 * b_heads <= SUBLANES
    assert seq % GDN_ROWS == 0 and seq % MOBA_BLOCK == 0 and dec_batch % SAMPLE_GROUP == 0
    assert taps - 1 + dec_seq <= SAMPLE_SLOT and dec_seq >= taps - 1 and seq >= taps - 1
    pad = SAMPLE_SLOT - (taps - 1) - dec_seq

    yp = x_prompt.reshape(batch * seq, d_model)
    ys = x_sample.reshape(dec_batch * dec_seq, d_model)
    outs = [[] for _ in range(8)]
    for l in range(depth):
        wg1, wu1 = w_ffn1_in[l, :, :d_ff].astype(BF16), w_ffn1_in[l, :, d_ff:].astype(BF16)
        wo1 = w_ffn1_out[l].astype(BF16)
        wg2, wu2 = w_ffn2_in[l, :, :d_ff].astype(BF16), w_ffn2_in[l, :, d_ff:].astype(BF16)
        wo2 = w_ffn2_out[l].astype(BF16)
        w_main = w_mix_in[l, :, :main].astype(BF16)
        w_tail = w_mix_in[l, :, main:]
        w_ba = jnp.pad(w_tail, ((0, 0), (0, LANES - 2 * b_heads))).astype(BF16)
        w_bat = jnp.pad(w_tail.T, ((0, SUBLANES - 2 * b_heads), (0, 0))).astype(BF16)
        wma, wmb = w_mix_out[l, :aw].astype(BF16), w_mix_out[l, aw:].astype(BF16)
        a_log, dt_bias = gdn_a_log[l].astype(F32), gdn_dt_bias[l].astype(F32)
        prow = jnp.zeros((SUBLANES, LANES), F32)
        prow = prow.at[0, b_heads:2 * b_heads].set(a_log).at[1, b_heads:2 * b_heads].set(dt_bias)
        pcol = jnp.zeros((SUBLANES, LANES), F32)
        pcol = pcol.at[b_heads:2 * b_heads, 0].set(a_log).at[b_heads:2 * b_heads, 1].set(dt_bias)
        convw = gdn_conv_w[l].astype(F32)
        gnorm = _row(gdn_norm_w[l])

        def trunk_in(x):
            h = _ffn_call(x, _row(norm_ffn1_pre[l]), wg1, wu1, wo1, _row(norm_ffn1_post[l]))
            return (h,) + tuple(_proj_call(h, _row(norm_mix_pre[l]), w_main, w_ba, w_bat, widths))

        def trunk_out(h, oa, ob):
            return _mix_ffn_call(h, oa, ob, wma, wmb, _row(norm_mix_post[l]), _row(norm_ffn2_pre[l]),
                                 wg2, wu2, wo2, _row(norm_ffn2_post[l]))

        hp, qa, ka, va, qkv, z, ba, bat = trunk_in(yp)
        oa = _moba_prompt_call(qa, ka, va, batch, seq, a_hd)
        ob, s_p = _gdn_prompt_call(qkv, z, ba, bat, convw, prow, pcol, gnorm, batch, seq, b_heads, b_hd)
        yp = trunk_out(hp, oa, ob)
        outs[0].append(ka.reshape(batch, seq, a_heads, a_hd))
        outs[1].append(va.reshape(batch, seq, a_heads, a_hd))
        outs[4].append(s_p)
        outs[6].append(qkv.reshape(batch, seq, conv_dim)[:, seq - (taps - 1):, :])

        hs, qa, ka, va, qkv, z, ba, bat = trunk_in(ys)
        seq3 = lambda a: a.reshape(dec_batch, dec_seq, a.shape[-1])
        slots = lambda a: jnp.pad(seq3(a), ((0, 0), (0, SAMPLE_SLOT - dec_seq), (0, 0)))
        pages_t = lambda c: jnp.transpose(c, (0, 1, 3, 4, 2)).reshape(c.shape[0], c.shape[1], aw, c.shape[2])
        oa = _moba_sample_call(seq3(qa), slots(ka), slots(va), pages_t(cache_k), pages_t(cache_v), l,
                               page_table, a_heads)
        qkv3 = seq3(qkv)
        slot_rows = lambda a: slots(a).reshape(dec_batch * SAMPLE_SLOT, a.shape[-1])
        xp = jnp.concatenate([state_conv[l].astype(F32), qkv3, jnp.zeros((dec_batch, pad, conv_dim), F32)],
                             axis=1).reshape(dec_batch * SAMPLE_SLOT, conv_dim)
        bat_slots = jnp.pad(bat.reshape(SUBLANES, dec_batch, dec_seq),
                            ((0, 0), (0, 0), (0, SAMPLE_SLOT - dec_seq))).reshape(SUBLANES, dec_batch * SAMPLE_SLOT)
        ob_slots, s_s = _gdn_sample_call(xp, slot_rows(z), slot_rows(ba), bat_slots, state_gdn[l].astype(F32),
                                         convw, prow, pcol, gnorm, b_heads, b_hd, dec_seq)
        ob = ob_slots.reshape(dec_batch, SAMPLE_SLOT, bw)[:, :dec_seq].reshape(dec_batch * dec_seq, bw)
        ys = trunk_out(hs, oa.reshape(dec_batch * dec_seq, aw), ob)
        outs[2].append(ka.reshape(dec_batch, dec_seq, a_heads, a_hd))
        outs[3].append(va.reshape(dec_batch, dec_seq, a_heads, a_hd))
        outs[5].append(s_s)
        outs[7].append(jnp.concatenate([state_conv[l].astype(F32), qkv3], axis=1)[:, dec_seq:, :])

    stacked = [jnp.stack(o) for o in outs]
    return (yp.reshape(batch, seq, d_model), ys.reshape(dec_batch, dec_seq, d_model), *stacked)
```

```python
import functools

import jax
import jax.numpy as jnp
from jax import lax
from jax.experimental import pallas as pl
from jax.experimental.pallas import tpu as pltpu

F32 = jnp.float32
BF16 = jnp.bfloat16

MOBA_BLOCK = 256
MOBA_TOPK = 3
GDN_CHUNK = 64
RMS_EPS = 1e-6
L2_EPS = 1e-6
NEG_INF = -1e30

LANES = 128
SUBLANES = 8
VMEM_LIMIT = 56 * 1024 * 1024

TOKEN_TILE = 512
FF_CHUNK = 256
GDN_ROWS = 256
SAMPLE_SLOT = 8
SAMPLE_GROUP = 16
PAGES_PER_CHUNK = 8
RING_SLOTS = 3


def _dot(a, b, precision=None):
    return jnp.dot(a, b, preferred_element_type=F32, precision=precision)


def _dot_nt(a, b, precision=None):
    return lax.dot_general(a, b, (((1,), (1,)), ((), ())), preferred_element_type=F32, precision=precision)


def _dot_tn(a, b, precision=None):
    return lax.dot_general(a, b, (((0,), (0,)), ((), ())), preferred_element_type=F32, precision=precision)


def _rms(x, w_row):
    ms = jnp.mean(x * x, axis=-1, keepdims=True)
    return x * lax.rsqrt(ms + RMS_EPS) * w_row


def _silu(x):
    return x * jax.nn.sigmoid(x)


def _softplus(x):
    return jnp.maximum(x, 0.0) + jnp.log1p(jnp.exp(-jnp.abs(x)))


def _const_spec(shape):
    return pl.BlockSpec(shape, lambda *_: (0,) * len(shape), pipeline_mode=pl.Buffered(1))


def _params(semantics):
    return pltpu.CompilerParams(dimension_semantics=semantics, vmem_limit_bytes=VMEM_LIMIT)


def _swiglu_residual(x, npre, wg_ref, wu_ref, wo_ref, npost):
    xn = _rms(x, npre).astype(BF16)
    d_ff = wg_ref.shape[1]
    acc = jnp.zeros(x.shape, F32)
    for c0 in range(0, d_ff, FF_CHUNK):
        gate = _dot(xn, wg_ref[:, c0:c0 + FF_CHUNK])
        up = _dot(xn, wu_ref[:, c0:c0 + FF_CHUNK])
        act = (_silu(gate) * up).astype(BF16)
        acc = acc + _dot(act, wo_ref[c0:c0 + FF_CHUNK, :])
    return x + 0.5 * _rms(acc, npost)


def _ffn_body(x_ref, npre_ref, wg_ref, wu_ref, wo_ref, npost_ref, o_ref):
    o_ref[...] = _swiglu_residual(x_ref[...], npre_ref[...], wg_ref, wu_ref, wo_ref, npost_ref[...])


def _mix_ffn_body(h_ref, oa_ref, ob_ref, wma_ref, wmb_ref, nmix_ref,
                  npre_ref, wg_ref, wu_ref, wo_ref, npost_ref, o_ref):
    mix = _dot(oa_ref[...].astype(BF16), wma_ref[...]) + _dot(ob_ref[...].astype(BF16), wmb_ref[...])
    h2 = h_ref[...] + _rms(mix, nmix_ref[...])
    o_ref[...] = _swiglu_residual(h2, npre_ref[...], wg_ref, wu_ref, wo_ref, npost_ref[...])


def _token_tile(n):
    return TOKEN_TILE if n % TOKEN_TILE == 0 else n


def _ffn_call(x, npre, wg, wu, wo, npost):
    n, d = x.shape
    tm = _token_tile(n)
    row = pl.BlockSpec((tm, d), lambda i: (i, 0))
    return pl.pallas_call(
        _ffn_body,
        out_shape=jax.ShapeDtypeStruct((n, d), F32),
        grid=(n // tm,),
        in_specs=[row, _const_spec(npre.shape), _const_spec(wg.shape), _const_spec(wu.shape),
                  _const_spec(wo.shape), _const_spec(npost.shape)],
        out_specs=row,
        compiler_params=_params(("parallel",)),
        name="ffn1",
    )(x, npre, wg, wu, wo, npost)


def _mix_ffn_call(h, oa, ob, wma, wmb, nmix, npre, wg, wu, wo, npost):
    n, d = h.shape
    tm = _token_tile(n)
    row = pl.BlockSpec((tm, d), lambda i: (i, 0))
    half = pl.BlockSpec((tm, oa.shape[1]), lambda i: (i, 0))
    consts = [wma, wmb, nmix, npre, wg, wu, wo, npost]
    return pl.pallas_call(
        _mix_ffn_body,
        out_shape=jax.ShapeDtypeStruct((n, d), F32),
        grid=(n // tm,),
        in_specs=[row, half, half] + [_const_spec(c.shape) for c in consts],
        out_specs=row,
        compiler_params=_params(("parallel",)),
        name="mix_ffn2",
    )(h, oa, ob, *consts)


def _proj_body(widths, h_ref, nrm_ref, w_ref, wba_ref, wbat_ref, *out_refs):
    a = _rms(h_ref[...], nrm_ref[...]).astype(BF16)
    c0 = 0
    for ref, width in zip(out_refs[:len(widths)], widths):
        ref[...] = _dot(a, w_ref[:, c0:c0 + width])
        c0 += width
    ba_ref, bat_ref = out_refs[len(widths):]
    ba_ref[...] = _dot(a, wba_ref[...])
    bat_ref[...] = _dot_nt(wbat_ref[...], a)


def _proj_call(h, nrm, w_main, w_ba, w_bat, widths):
    n, d = h.shape
    tm = _token_tile(n)
    outs = [jax.ShapeDtypeStruct((n, w), F32) for w in widths]
    outs += [jax.ShapeDtypeStruct((n, LANES), F32), jax.ShapeDtypeStruct((SUBLANES, n), F32)]
    out_specs = [pl.BlockSpec((tm, w), lambda i: (i, 0)) for w in widths]
    out_specs += [pl.BlockSpec((tm, LANES), lambda i: (i, 0)), pl.BlockSpec((SUBLANES, tm), lambda i: (0, i))]
    return pl.pallas_call(
        functools.partial(_proj_body, tuple(widths)),
        out_shape=outs,
        grid=(n // tm,),
        in_specs=[pl.BlockSpec((tm, d), lambda i: (i, 0)), _const_spec(nrm.shape), _const_spec(w_main.shape),
                  _const_spec(w_ba.shape), _const_spec(w_bat.shape)],
        out_specs=out_specs,
        compiler_params=_params(("parallel",)),
        name="mix_in_proj",
    )(h, nrm, w_main, w_ba, w_bat)


def _top_blocks_t(scores, valid):
    row = lax.broadcasted_iota(jnp.int32, scores.shape, 0)
    s = jnp.where(valid, scores, NEG_INF)
    rank = jnp.zeros(scores.shape, jnp.int32)
    for m in range(scores.shape[0]):
        sm = s[m:m + 1, :]
        ahead = (sm > s) | ((sm == s) & (row > m))
        rank = rank + jnp.where(ahead, 1, 0)
    return valid & (rank < MOBA_TOPK)


def _moba_prompt_body(head_dim, q_ref, k_ref, v_ref, o_ref, kbf_ref, vt_ref, kmean_ref, sel_ref):
    j = pl.program_id(2)
    n_blocks = vt_ref.shape[0]
    n_heads = LANES // head_dim
    blk = MOBA_BLOCK

    @pl.when(j == 0)
    def _():
        kbf_ref[...] = k_ref[...].astype(BF16)
        for n in range(n_blocks):
            rows = slice(n * blk, (n + 1) * blk)
            vt_ref[n] = v_ref[rows, :].T.astype(BF16)
            kmean_ref[n:n + 1, :] = jnp.mean(k_ref[rows, :], axis=0, keepdims=True)

    q = q_ref[...] * (head_dim ** -0.5)
    lane = lax.broadcasted_iota(jnp.int32, q.shape, 1)
    key_i = lax.broadcasted_iota(jnp.int32, (blk, blk), 0)
    qry_i = lax.broadcasted_iota(jnp.int32, (blk, blk), 1)
    causal = key_i <= qry_i
    cand = lax.broadcasted_iota(jnp.int32, (n_blocks, blk), 0)
    kmean = kmean_ref[...].astype(BF16)
    k_own = kbf_ref[pl.ds(pl.multiple_of(j * blk, blk), blk), :]

    qhs = []
    for hh in range(n_heads):
        in_head = (lane >= hh * head_dim) & (lane < (hh + 1) * head_dim)
        qhs.append(jnp.where(in_head, q, 0.0).astype(BF16))
    gates = [_dot_nt(kmean, qh) for qh in qhs]
    own_lg = [_dot_nt(k_own, qh) for qh in qhs]
    for hh in range(n_heads):
        sel_ref[hh] = jnp.where(_top_blocks_t(gates[hh], cand < j), 1.0, 0.0)

    def softmax_step(lg, m_run, l_run):
        m_new = jnp.maximum(m_run, jnp.max(lg, axis=0, keepdims=True))
        alpha = jnp.exp(m_run - m_new)
        p = jnp.exp(lg - m_new)
        return m_new, alpha, alpha * l_run + jnp.sum(p, axis=0, keepdims=True), p.astype(BF16)

    def values_t(n, hh):
        return vt_ref[n, hh * head_dim:(hh + 1) * head_dim, :]

    carry = []
    m_init, l_init = jnp.full((1, blk), NEG_INF, F32), jnp.zeros((1, blk), F32)
    own = [softmax_step(jnp.where(causal, own_lg[hh], NEG_INF), m_init, l_init) for hh in range(n_heads)]
    for hh in range(n_heads):
        m0, _, l0, p0 = own[hh]
        carry += [m0, l0, _dot(values_t(j, hh), p0)]

    def past_block(n, carry):
        kb = kbf_ref[pl.ds(pl.multiple_of(n * blk, blk), blk), :]
        lgs = [_dot_nt(kb, qh) for qh in qhs]
        steps = []
        for hh in range(n_heads):
            picked = sel_ref[hh, pl.ds(n, 1), :] > 0.5
            steps.append(softmax_step(jnp.where(picked, lgs[hh], NEG_INF), carry[3 * hh], carry[3 * hh + 1]))
        out = []
        for hh in range(n_heads):
            m_new, alpha, l_new, p16 = steps[hh]
            out += [m_new, l_new, alpha * carry[3 * hh + 2] + _dot(values_t(n, hh), p16)]
        return tuple(out)

    carry = lax.fori_loop(0, j, past_block, tuple(carry))
    out_t = jnp.concatenate([carry[3 * hh + 2] / carry[3 * hh + 1] for hh in range(n_heads)], axis=0)
    o_ref[...] = out_t.T


def _moba_prompt_call(q, k, v, batch, seq, head_dim):
    n, width = q.shape
    pairs = width // LANES
    nb = seq // MOBA_BLOCK
    assert nb <= SUBLANES
    qspec = pl.BlockSpec((MOBA_BLOCK, LANES), lambda b, hp, j: (b * nb + j, hp))
    kvspec = pl.BlockSpec((seq, LANES), lambda b, hp, j: (b, hp))
    return pl.pallas_call(
        functools.partial(_moba_prompt_body, head_dim),
        out_shape=jax.ShapeDtypeStruct((n, width), F32),
        grid=(batch, pairs, nb),
        in_specs=[qspec, kvspec, kvspec],
        out_specs=qspec,
        scratch_shapes=[pltpu.VMEM((seq, LANES), BF16),
                        pltpu.VMEM((nb, LANES, MOBA_BLOCK), BF16),
                        pltpu.VMEM((nb, LANES), F32),
                        pltpu.VMEM((LANES // head_dim, nb, MOBA_BLOCK), F32)],
        compiler_params=_params(("parallel", "parallel", "arbitrary")),
        name="moba_prompt",
    )(q, k, v)


def _moba_sample_body(layer, n_new, n_heads, pt_ref, q_ref, kn_ref, vn_ref, k_hbm, v_hbm, o_ref,
                      kbuf, vbuf, sem, ot_ref):
    s = pl.program_id(0)
    n_seq = pl.num_programs(0)
    _, cp, width, page = kbuf.shape
    hd = width // n_heads
    n_chunks = pt_ref.shape[1] // cp
    pages_per_block = MOBA_BLOCK // page
    blocks_per_chunk = cp // pages_per_block
    n_blocks = n_chunks * blocks_per_chunk
    rows = LANES

    def page_copies(seq, chunk, slot, i):
        page_id = pt_ref[seq, chunk * cp + i]
        return (pltpu.make_async_copy(k_hbm.at[layer, page_id], kbuf.at[slot, i], sem.at[slot]),
                pltpu.make_async_copy(v_hbm.at[layer, page_id], vbuf.at[slot, i], sem.at[slot]))

    def start_chunk(seq, chunk, slot):
        for i in range(cp):
            for thread, copy in enumerate(page_copies(seq, chunk, slot, i)):
                copy.start(priority=thread)

    def wait_chunk(slot):
        for i in range(cp):
            for copy in page_copies(0, 0, slot, i):
                copy.wait()

    @pl.when(s == 0)
    def _():
        for c in range(RING_SLOTS - 1):
            start_chunk(0, c, c)

    q = q_ref[0] * (hd ** -0.5)
    lane_w = lax.broadcasted_iota(jnp.int32, (n_heads, width), 1)
    head_w = lax.broadcasted_iota(jnp.int32, (n_heads, width), 0)
    in_head = (lane_w >= head_w * hd) & (lane_w < (head_w + 1) * hd)
    q_rows = [jnp.where(in_head, jnp.broadcast_to(q[t:t + 1, :], (n_heads, width)), 0.0) for t in range(n_new)]
    q_rows.append(jnp.zeros((rows - n_new * n_heads, width), F32))
    qm = jnp.concatenate(q_rows, axis=0).astype(BF16)
    col_head = lax.broadcasted_iota(jnp.int32, (hd, rows), 1) & (n_heads - 1)
    lane = lax.broadcasted_iota(jnp.int32, (rows, LANES), 1)

    def block_stats(kt, vt, mask):
        logits = _dot(qm, kt)
        if mask is not None:
            logits = jnp.where(mask, logits, NEG_INF)
        g = jnp.sum(logits, axis=-1, keepdims=True)
        m = jnp.max(logits, axis=-1, keepdims=True)
        p = jnp.exp(logits - m)
        l = jnp.sum(p, axis=-1, keepdims=True)
        ot_all = _dot_nt(vt, p.astype(BF16))
        ot = jnp.zeros((hd, rows), F32)
        for h in range(n_heads):
            ot = jnp.where(col_head == h, ot_all[h * hd:(h + 1) * hd, :], ot)
        return g, m, l, ot

    def put(stat, b, column):
        return jnp.where(lane == b, column, stat)

    ahead = RING_SLOTS - 1

    def chunk_step(c, stats):
        g = s * n_chunks + c
        slot = lax.rem(g, RING_SLOTS)
        wait_chunk(slot)
        fill = lax.rem(g + ahead, RING_SLOTS)

        @pl.when(c + ahead < n_chunks)
        def _():
            start_chunk(s, c + ahead, fill)

        @pl.when((c + ahead >= n_chunks) & (s + 1 < n_seq))
        def _():
            start_chunk(s + 1, c + ahead - n_chunks, fill)

        g_all, m_all, l_all = stats
        for j in range(blocks_per_chunk):
            pages = range(j * pages_per_block, (j + 1) * pages_per_block)
            kt = jnp.concatenate([kbuf[slot, i] for i in pages], axis=1).astype(BF16)
            vt = jnp.concatenate([vbuf[slot, i] for i in pages], axis=1).astype(BF16)
            g, m, l, ot = block_stats(kt, vt, None)
            b = c * blocks_per_chunk + j
            ot_ref[b] = ot
            g_all, m_all, l_all = put(g_all, b, g), put(m_all, b, m), put(l_all, b, l)
        return g_all, m_all, l_all

    init = (jnp.zeros((rows, LANES), F32),) * 3
    g_all, m_all, l_all = lax.fori_loop(0, n_chunks, chunk_step, init)

    pad_t = lambda a: jnp.concatenate([a, jnp.zeros((LANES - a.shape[0], width), F32)], axis=0).T.astype(BF16)
    row = lax.broadcasted_iota(jnp.int32, (rows, LANES), 0)
    own_ok = (lane * n_heads <= row) & (lane < n_new)
    _, m_own, l_own, ot_own = block_stats(pad_t(kn_ref[0]), pad_t(vn_ref[0]), own_ok)
    m_all, l_all = put(m_all, n_blocks, m_own), put(l_all, n_blocks, l_own)

    gt, mt, lt = g_all.T[:n_blocks], m_all.T, l_all.T
    sel = _top_blocks_t(gt, lax.broadcasted_iota(jnp.int32, gt.shape, 0) < n_blocks)
    m_own_t, l_own_t = mt[n_blocks:n_blocks + 1], lt[n_blocks:n_blocks + 1]
    mt, lt = mt[:n_blocks], lt[:n_blocks]
    m_fin = jnp.maximum(jnp.max(jnp.where(sel, mt, NEG_INF), axis=0, keepdims=True), m_own_t)
    w = jnp.where(sel, jnp.exp(mt - m_fin), 0.0)
    w_own = jnp.exp(m_own_t - m_fin)
    denom = jnp.sum(w * lt, axis=0, keepdims=True) + w_own * l_own_t
    out_t = w_own * ot_own
    for b in range(n_blocks):
        out_t = out_t + w[b:b + 1, :] * ot_ref[b]
    out = (out_t / denom).T
    o_ref[0] = out[:n_new * n_heads].reshape(n_new, n_heads, hd)


def _moba_sample_call(q, kn, vn, pool_kt, pool_vt, layer, page_table, n_heads):
    seqs, n_new, width = q.shape
    page = pool_kt.shape[3]
    hd = width // n_heads
    n_pages = page_table.shape[1]
    cp = PAGES_PER_CHUNK
    n_chunks = n_pages // cp
    n_blocks = n_pages * page // MOBA_BLOCK
    assert n_pages % cp == 0 and n_chunks >= RING_SLOTS - 1
    assert MOBA_BLOCK % page == 0 and cp % (MOBA_BLOCK // page) == 0 and page == LANES
    assert n_blocks < LANES and n_new * n_heads <= LANES and n_new <= SAMPLE_SLOT

    per_seq = lambda a: pl.BlockSpec((1,) + a.shape[1:], lambda s, pt: (s, 0, 0))
    grid_spec = pltpu.PrefetchScalarGridSpec(
        num_scalar_prefetch=1,
        grid=(seqs,),
        in_specs=[per_seq(q), per_seq(kn), per_seq(vn),
                  pl.BlockSpec(memory_space=pl.ANY), pl.BlockSpec(memory_space=pl.ANY)],
        out_specs=pl.BlockSpec((1, n_new, n_heads, hd), lambda s, pt: (s, 0, 0, 0)),
        scratch_shapes=[pltpu.VMEM((RING_SLOTS, cp, width, page), F32),
                        pltpu.VMEM((RING_SLOTS, cp, width, page), F32),
                        pltpu.SemaphoreType.DMA((RING_SLOTS,)),
                        pltpu.VMEM((n_blocks, hd, LANES), F32)],
    )
    return pl.pallas_call(
        functools.partial(_moba_sample_body, layer, n_new, n_heads),
        out_shape=jax.ShapeDtypeStruct((seqs, n_new, n_heads, hd), F32),
        grid_spec=grid_spec,
        compiler_params=_params(("arbitrary",)),
        name="moba_sample",
    )(page_table, q, kn, vn, pool_kt, pool_vt)


def _unit_lower_inverses(l_stricts, chunk):
    n = l_stricts[0].shape[0]
    eye = jnp.where(lax.broadcasted_iota(jnp.int32, (n, n), 0) == lax.broadcasted_iota(jnp.int32, (n, n), 1),
                    1.0, 0.0)
    ms = [-l for l in l_stricts]
    ss = [eye + m for m in ms]
    ps = [_dot(m.astype(BF16), m.astype(BF16)) for m in ms]
    steps = chunk.bit_length() - 2
    for it in range(steps):
        p16s = [p.astype(BF16) for p in ps]
        ss = [s + _dot(p16, s.astype(BF16)) for s, p16 in zip(ss, p16s)]
        if it + 1 < steps:
            ps = [_dot(p16, p16) for p16 in p16s]
    return ss


def _gdn_prepare(xs_ref, base, rows, convw_ref, ba_ref, bat_ref, prow_ref, pcol_ref, chunk, n_heads, hd, valid):
    w = convw_ref[...]
    taps = w.shape[0]
    y = xs_ref[pl.ds(base, rows), :] * w[0:1, :]
    for t in range(1, taps):
        y = y + xs_ref[pl.ds(base + t, rows), :] * w[t:t + 1, :]
    y = _silu(y)
    bw = n_heads * hd
    ba = ba_ref[...]
    bat = bat_ref[...]
    beta_all = jax.nn.sigmoid(ba)
    g_all = -jnp.exp(prow_ref[0:1, :]) * _softplus(ba + prow_ref[1:2, :])
    g_rows = -jnp.exp(pcol_ref[:, 0:1]) * _softplus(bat + pcol_ref[:, 1:2])
    if valid is not None:
        vcol, vrow = valid
        beta_all = jnp.where(vcol, beta_all, 0.0)
        g_all = jnp.where(vcol, g_all, 0.0)
        g_rows = jnp.where(vrow, g_rows, 0.0)
    ri = lax.broadcasted_iota(jnp.int32, (rows, rows), 0)
    ci = lax.broadcasted_iota(jnp.int32, (rows, rows), 1)
    shift = chunk.bit_length() - 1
    same = (ri >> shift) == (ci >> shift)
    lower = jnp.where(same & (ci <= ri), 1.0, 0.0)
    upper = jnp.where(same & (ri <= ci), 1.0, 0.0)
    hi = lax.Precision.HIGHEST
    gc_cols = _dot(lower, g_all, hi)
    gc_rows = _dot(g_rows, upper, hi)
    heads = []
    for h in range(n_heads):
        q = y[:, h * hd:(h + 1) * hd]
        k = y[:, bw + h * hd:bw + (h + 1) * hd]
        v = y[:, 2 * bw + h * hd:2 * bw + (h + 1) * hd]
        q = q * lax.rsqrt(jnp.sum(q * q, axis=-1, keepdims=True) + L2_EPS) * (hd ** -0.5)
        k = k * lax.rsqrt(jnp.sum(k * k, axis=-1, keepdims=True) + L2_EPS)
        if valid is not None:
            k = jnp.where(valid[0], k, 0.0)
            v = jnp.where(valid[0], v, 0.0)
        beta = beta_all[:, h:h + 1]
        gcc = gc_cols[:, n_heads + h:n_heads + h + 1]
        gcr = gc_rows[n_heads + h:n_heads + h + 1, :]
        heads.append((q, k, v, beta, gcc, gcr))
    return heads, same, ri, ci


def _gdn_matrices(heads, same, ri, ci, chunk):
    causal = same & (ci <= ri)
    strict = same & (ci < ri)
    hd = heads[0][0].shape[1]
    decays, kbs, k16s, lmats, attns = [], [], [], [], []
    for q, k, v, beta, gcc, gcr in heads:
        decays.append(jnp.where(causal, jnp.exp(jnp.where(causal, gcc - gcr, 0.0)), 0.0))
        kbs.append(k * beta)
        k16s.append(k.astype(BF16))
    for (q, k, v, beta, gcc, gcr), decay, kb, k16 in zip(heads, decays, kbs, k16s):
        lmats.append(jnp.where(strict, _dot_nt(kb.astype(BF16), k16) * decay, 0.0))
        attns.append((_dot_nt(q.astype(BF16), k16) * decay).astype(BF16))
    tinvs = _unit_lower_inverses(lmats, chunk)
    out = []
    for (q, k, v, beta, gcc, gcr), kb, tinv, attn16 in zip(heads, kbs, tinvs, attns):
        egc = jnp.exp(gcc)
        rhs = jnp.concatenate([v * beta, kb * egc], axis=1).astype(BF16)
        uw = _dot(tinv.astype(BF16), rhs)
        out.append((uw[:, :hd], uw[:, hd:], attn16, q * egc))
    return out


def _gated_out(o, gnorm_row, z):
    return _rms(o, gnorm_row) * _silu(z)


def _gdn_prompt_body(n_heads, hd, qkv_ref, z_ref, ba_ref, bat_ref, convw_ref, prow_ref, pcol_ref, gnorm_ref,
                     o_ref, sfin_ref, xs_ref, s_ref, vn_ref):
    i = pl.program_id(1)
    rows = qkv_ref.shape[0]
    chunk = GDN_CHUNK

    @pl.when(i == 0)
    def _():
        xs_ref[0:SUBLANES, :] = jnp.zeros((SUBLANES, xs_ref.shape[1]), F32)
        s_ref[...] = jnp.zeros(s_ref.shape, F32)

    x = qkv_ref[...]
    xs_ref[SUBLANES:SUBLANES + rows, :] = x
    taps = convw_ref.shape[0]
    heads, same, ri, ci = _gdn_prepare(xs_ref, SUBLANES - taps + 1, rows, convw_ref, ba_ref, bat_ref, prow_ref,
                                       pcol_ref, chunk, n_heads, hd, None)
    xs_ref[0:SUBLANES, :] = x[rows - SUBLANES:rows, :]

    gnorm = gnorm_ref[...]
    mats = _gdn_matrices(heads, same, ri, ci, chunk)
    vn_ref[...] = jnp.zeros(vn_ref.shape, BF16)
    states = [s_ref[h] for h in range(n_heads)]
    for c in range(rows // chunk):
        r0 = c * chunk
        sl = slice(r0, r0 + chunk)
        s16s = [s.astype(BF16) for s in states]
        v_news = [u[sl] - _dot(w[sl].astype(BF16), s16) for (u, w, _, _), s16 in zip(mats, s16s)]
        for h, v_new in enumerate(v_news):
            vn_ref[h, sl, :] = v_new.astype(BF16)
        outs = [_dot(qg[sl].astype(BF16), s16) + _dot(attn16[sl, :], vn_ref[h])
                for h, ((_, _, attn16, qg), s16) in enumerate(zip(mats, s16s))]
        for h, ((q, k, v, beta, gcc, gcr), v_new) in enumerate(zip(heads, v_news)):
            gl = gcc[r0 + chunk - 1:r0 + chunk, :]
            kd = (k[sl] * jnp.exp(gl - gcc[sl])).astype(BF16)
            states[h] = states[h] * jnp.exp(gl) + _dot_tn(kd, v_new.astype(BF16))
        for h, o in enumerate(outs):
            o_ref[sl, h * hd:(h + 1) * hd] = _gated_out(o, gnorm, z_ref[sl, h * hd:(h + 1) * hd])
    for h in range(n_heads):
        s_ref[h] = states[h]

    @pl.when(i == pl.num_programs(1) - 1)
    def _():
        sfin_ref[0] = s_ref[...]


def _gdn_prompt_call(qkv, z, ba, bat, convw, prow, pcol, gnorm, batch, seq, n_heads, hd):
    n, cdim = qkv.shape
    rows = GDN_ROWS
    nblk = seq // rows
    bw = n_heads * hd
    tile = lambda width: pl.BlockSpec((rows, width), lambda b, i: (b * nblk + i, 0))
    return pl.pallas_call(
        functools.partial(_gdn_prompt_body, n_heads, hd),
        out_shape=[jax.ShapeDtypeStruct((n, bw), F32), jax.ShapeDtypeStruct((batch, n_heads, hd, hd), F32)],
        grid=(batch, nblk),
        in_specs=[tile(cdim), tile(bw), tile(LANES),
                  pl.BlockSpec((SUBLANES, rows), lambda b, i: (0, b * nblk + i)),
                  _const_spec(convw.shape), _const_spec(prow.shape), _const_spec(pcol.shape),
                  _const_spec(gnorm.shape)],
        out_specs=[tile(bw), pl.BlockSpec((1, n_heads, hd, hd), lambda b, i: (b, 0, 0, 0))],
        scratch_shapes=[pltpu.VMEM((SUBLANES + rows, cdim), F32), pltpu.VMEM((n_heads, hd, hd), F32),
                        pltpu.VMEM((n_heads, rows, hd), BF16)],
        compiler_params=_params(("parallel", "arbitrary")),
        name="gdn_prompt",
    )(qkv, z, ba, bat, convw, prow, pcol, gnorm)


def _gdn_sample_body(n_heads, hd, n_new, xp_ref, z_ref, ba_ref, bat_ref, s0_ref, convw_ref, prow_ref, pcol_ref,
                     gnorm_ref, o_ref, snew_ref, xs_ref):
    rows = xp_ref.shape[0]
    slot = SAMPLE_SLOT
    xs_ref[0:rows, :] = xp_ref[...]
    xs_ref[rows:rows + SUBLANES, :] = jnp.zeros((SUBLANES, xs_ref.shape[1]), F32)
    vcol = (lax.broadcasted_iota(jnp.int32, (rows, 1), 0) & (slot - 1)) < n_new
    vrow = (lax.broadcasted_iota(jnp.int32, (1, rows), 1) & (slot - 1)) < n_new
    heads, same, ri, ci = _gdn_prepare(xs_ref, 0, rows, convw_ref, ba_ref, bat_ref, prow_ref, pcol_ref,
                                       slot, n_heads, hd, (vcol, vrow))
    gnorm = gnorm_ref[...]
    mats = _gdn_matrices(heads, same, ri, ci, slot)
    for h, ((q, k, v, beta, gcc, gcr), (u, w, attn16, qg)) in enumerate(zip(heads, mats)):
        v_parts, qs_parts = [], []
        for g in range(rows // slot):
            r0 = g * slot
            s16 = s0_ref[g, h].astype(BF16)
            lhs = jnp.concatenate([w[r0:r0 + slot], qg[r0:r0 + slot]], axis=0).astype(BF16)
            both = _dot(lhs, s16)
            v_parts.append(u[r0:r0 + slot] - both[:slot])
            qs_parts.append(both[slot:])
        v_new = jnp.concatenate(v_parts, axis=0)
        o = jnp.concatenate(qs_parts, axis=0) + _dot(attn16, v_new.astype(BF16))
        o_ref[:, h * hd:(h + 1) * hd] = _gated_out(o, gnorm, z_ref[:, h * hd:(h + 1) * hd])
        for g in range(rows // slot):
            r0 = g * slot
            gl = gcc[r0 + slot - 1:r0 + slot, :]
            kd = k[r0:r0 + slot] * jnp.exp(gl - gcc[r0:r0 + slot])
            snew_ref[g, h] = s0_ref[g, h] * jnp.exp(gl) + _dot_tn(kd, v_new[r0:r0 + slot])


def _gdn_sample_call(xp, z, ba, bat, s0, convw, prow, pcol, gnorm, n_heads, hd, n_new):
    n, cdim = xp.shape
    rows = SAMPLE_GROUP * SAMPLE_SLOT
    bw = n_heads * hd
    tile = lambda width: pl.BlockSpec((rows, width), lambda i: (i, 0))
    sspec = pl.BlockSpec((SAMPLE_GROUP, n_heads, hd, hd), lambda i: (i, 0, 0, 0))
    return pl.pallas_call(
        functools.partial(_gdn_sample_body, n_heads, hd, n_new),
        out_shape=[jax.ShapeDtypeStruct((n, bw), F32), jax.ShapeDtypeStruct(s0.shape, F32)],
        grid=(n // rows,),
        in_specs=[tile(cdim), tile(bw), tile(LANES), pl.BlockSpec((SUBLANES, rows), lambda i: (0, i)), sspec,
                  _const_spec(convw.shape), _const_spec(prow.shape), _const_spec(pcol.shape),
                  _const_spec(gnorm.shape)],
        out_specs=[tile(bw), sspec],
        scratch_shapes=[pltpu.VMEM((rows + SUBLANES, cdim), F32)],
        compiler_params=_params(("parallel",)),
        name="gdn_sample",
    )(xp, z, ba, bat, s0, convw, prow, pcol, gnorm)


def _row(v):
    return v.reshape(1, -1).astype(F32)


def kernel(x_prompt, x_sample, cache_k, cache_v, state_gdn, state_conv, page_table, norm_ffn1_pre, norm_ffn1_post, w_ffn1_in, w_ffn1_out, norm_mix_pre, norm_mix_post, w_mix_in, w_mix_out, gdn_conv_w, gdn_a_log, gdn_dt_bias, gdn_norm_w, norm_ffn2_pre, norm_ffn2_post, w_ffn2_in, w_ffn2_out):
    batch, seq, d_model = x_prompt.shape
    dec_batch, dec_seq, _ = x_sample.shape
    depth = w_ffn1_in.shape[0]
    d_ff = w_ffn1_out.shape[1]
    a_heads, a_hd = cache_k.shape[3], cache_k.shape[4]
    b_heads, b_hd = state_gdn.shape[2], state_gdn.shape[3]
    aw, bw = a_heads * a_hd, b_heads * b_hd
    conv_dim = gdn_conv_w.shape[2]
    taps = gdn_conv_w.shape[1]
    widths = (aw, aw, aw, conv_dim, bw)
    main = sum(widths)
    assert w_mix_in.shape[2] == main + 2 * b_heads and 2 * b_heads <= SUBLANES
    assert seq % GDN_ROWS == 0 and seq % MOBA_BLOCK == 0 and dec_batch % SAMPLE_GROUP == 0
    assert taps - 1 + dec_seq <= SAMPLE_SLOT and dec_seq >= taps - 1 and seq >= taps - 1
    pad = SAMPLE_SLOT - (taps - 1) - dec_seq

    yp = x_prompt.reshape(batch * seq, d_model)
    ys = x_sample.reshape(dec_batch * dec_seq, d_model)
    outs = [[] for _ in range(8)]
    for l in range(depth):
        wg1, wu1 = w_ffn1_in[l, :, :d_ff].astype(BF16), w_ffn1_in[l, :, d_ff:].astype(BF16)
        wo1 = w_ffn1_out[l].astype(BF16)
        wg2, wu2 = w_ffn2_in[l, :, :d_ff].astype(BF16), w_ffn2_in[l, :, d_ff:].astype(BF16)
        wo2 = w_ffn2_out[l].astype(BF16)
        w_main = w_mix_in[l, :, :main].astype(BF16)
        w_tail = w_mix_in[l, :, main:]
        w_ba = jnp.pad(w_tail, ((0, 0), (0, LANES - 2 * b_heads))).astype(BF16)
        w_bat = jnp.pad(w_tail.T, ((0, SUBLANES - 2 * b_heads), (0, 0))).astype(BF16)
        wma, wmb = w_mix_out[l, :aw].astype(BF16), w_mix_out[l, aw:].astype(BF16)
        a_log, dt_bias = gdn_a_log[l].astype(F32), gdn_dt_bias[l].astype(F32)
        prow = jnp.zeros((SUBLANES, LANES), F32)
        prow = prow.at[0, b_heads:2 * b_heads].set(a_log).at[1, b_heads:2 * b_heads].set(dt_bias)
        pcol = jnp.zeros((SUBLANES, LANES), F32)
        pcol = pcol.at[b_heads:2 * b_heads, 0].set(a_log).at[b_heads:2 * b_heads, 1].set(dt_bias)
        convw = gdn_conv_w[l].astype(F32)
        gnorm = _row(gdn_norm_w[l])

        def trunk_in(x):
            h = _ffn_call(x, _row(norm_ffn1_pre[l]), wg1, wu1, wo1, _row(norm_ffn1_post[l]))
            return (h,) + tuple(_proj_call(h, _row(norm_mix_pre[l]), w_main, w_ba, w_bat, widths))

        def trunk_out(h, oa, ob):
            return _mix_ffn_call(h, oa, ob, wma, wmb, _row(norm_mix_post[l]), _row(norm_ffn2_pre[l]),
                                 wg2, wu2, wo2, _row(norm_ffn2_post[l]))

        hp, qa, ka, va, qkv, z, ba, bat = trunk_in(yp)
        oa = _moba_prompt_call(qa, ka, va, batch, seq, a_hd)
        ob, s_p = _gdn_prompt_call(qkv, z, ba, bat, convw, prow, pcol, gnorm, batch, seq, b_heads, b_hd)
        yp = trunk_out(hp, oa, ob)
        outs[0].append(ka.reshape(batch, seq, a_heads, a_hd))
        outs[1].append(va.reshape(batch, seq, a_heads, a_hd))
        outs[4].append(s_p)
        outs[6].append(qkv.reshape(batch, seq, conv_dim)[:, seq - (taps - 1):, :])

        hs, qa, ka, va, qkv, z, ba, bat = trunk_in(ys)
        seq3 = lambda a: a.reshape(dec_batch, dec_seq, a.shape[-1])
        slots = lambda a: jnp.pad(seq3(a), ((0, 0), (0, SAMPLE_SLOT - dec_seq), (0, 0)))
        pages_t = lambda c: jnp.transpose(c, (0, 1, 3, 4, 2)).reshape(c.shape[0], c.shape[1], aw, c.shape[2])
        oa = _moba_sample_call(seq3(qa), slots(ka), slots(va), pages_t(cache_k), pages_t(cache_v), l,
                               page_table, a_heads)
        qkv3 = seq3(qkv)
        slot_rows = lambda a: slots(a).reshape(dec_batch * SAMPLE_SLOT, a.shape[-1])
        xp = jnp.concatenate([state_conv[l].astype(F32), qkv3, jnp.zeros((dec_batch, pad, conv_dim), F32)],
                             axis=1).reshape(dec_batch * SAMPLE_SLOT, conv_dim)
        bat_slots = jnp.pad(bat.reshape(SUBLANES, dec_batch, dec_seq),
                            ((0, 0), (0, 0), (0, SAMPLE_SLOT - dec_seq))).reshape(SUBLANES, dec_batch * SAMPLE_SLOT)
        ob_slots, s_s = _gdn_sample_call(xp, slot_rows(z), slot_rows(ba), bat_slots, state_gdn[l].astype(F32),
                                         convw, prow, pcol, gnorm, b_heads, b_hd, dec_seq)
        ob = ob_slots.reshape(dec_batch, SAMPLE_SLOT, bw)[:, :dec_seq].reshape(dec_batch * dec_seq, bw)
        ys = trunk_out(hs, oa.reshape(dec_batch * dec_seq, aw), ob)
        outs[2].append(ka.reshape(dec_batch, dec_seq, a_heads, a_hd))
        outs[3].append(va.reshape(dec_batch, dec_seq, a_heads, a_hd))
        outs[5].append(s_s)
        outs[7].append(jnp.concatenate([state_conv[l].astype(F32), qkv3], axis=1)[:, dec_seq:, :])

    stacked = [jnp.stack(o) for o in outs]
    return (yp.reshape(batch, seq, d_model), ys.reshape(dec_batch, dec_seq, d_model), *stacked)
```

```python
import functools

import jax
import jax.numpy as jnp
from jax import lax
from jax.experimental import pallas as pl
from jax.experimental.pallas import tpu as pltpu

F32 = jnp.float32
BF16 = jnp.bfloat16

MOBA_BLOCK = 256
MOBA_TOPK = 3
MOBA_HEAD_LANES = 256
GDN_CHUNK = 64
RMS_EPS = 1e-6
L2_EPS = 1e-6
NEG_INF = -1e30

LANES = 128
SUBLANES = 8
VMEM_LIMIT = 56 * 1024 * 1024

TOKEN_TILE = 512
FF_CHUNK = 256
GDN_ROWS = 256
SAMPLE_SLOT = 8
SAMPLE_GROUP = 16
PAGES_PER_CHUNK = 8
RING_SLOTS = 3


def _dot(a, b, precision=None):
    return jnp.dot(a, b, preferred_element_type=F32, precision=precision)


def _dot_nt(a, b, precision=None):
    return lax.dot_general(a, b, (((1,), (1,)), ((), ())), preferred_element_type=F32, precision=precision)


def _dot_tn(a, b, precision=None):
    return lax.dot_general(a, b, (((0,), (0,)), ((), ())), preferred_element_type=F32, precision=precision)


def _rms(x, w_row):
    ms = jnp.mean(x * x, axis=-1, keepdims=True)
    return x * lax.rsqrt(ms + RMS_EPS) * w_row


def _silu(x):
    return x * jax.nn.sigmoid(x)


def _softplus(x):
    return jnp.maximum(x, 0.0) + jnp.log1p(jnp.exp(-jnp.abs(x)))


def _const_spec(shape):
    return pl.BlockSpec(shape, lambda *_: (0,) * len(shape), pipeline_mode=pl.Buffered(1))


def _params(semantics):
    return pltpu.CompilerParams(dimension_semantics=semantics, vmem_limit_bytes=VMEM_LIMIT)


def _swiglu_residual(x, npre, wg_ref, wu_ref, wo_ref, npost):
    xn = _rms(x, npre).astype(BF16)
    d_ff = wg_ref.shape[1]
    acc = jnp.zeros(x.shape, F32)
    for c0 in range(0, d_ff, FF_CHUNK):
        gate = _dot(xn, wg_ref[:, c0:c0 + FF_CHUNK])
        up = _dot(xn, wu_ref[:, c0:c0 + FF_CHUNK])
        act = (_silu(gate) * up).astype(BF16)
        acc = acc + _dot(act, wo_ref[c0:c0 + FF_CHUNK, :])
    return x + 0.5 * _rms(acc, npost)


def _ffn_body(x_ref, npre_ref, wg_ref, wu_ref, wo_ref, npost_ref, o_ref):
    o_ref[...] = _swiglu_residual(x_ref[...], npre_ref[...], wg_ref, wu_ref, wo_ref, npost_ref[...])


def _mix_ffn_body(h_ref, oa_ref, ob_ref, wma_ref, wmb_ref, nmix_ref,
                  npre_ref, wg_ref, wu_ref, wo_ref, npost_ref, o_ref):
    mix = _dot(oa_ref[...].astype(BF16), wma_ref[...]) + _dot(ob_ref[...].astype(BF16), wmb_ref[...])
    h2 = h_ref[...] + _rms(mix, nmix_ref[...])
    o_ref[...] = _swiglu_residual(h2, npre_ref[...], wg_ref, wu_ref, wo_ref, npost_ref[...])


def _token_tile(n):
    return TOKEN_TILE if n % TOKEN_TILE == 0 else n


def _ffn_call(x, npre, wg, wu, wo, npost):
    n, d = x.shape
    tm = _token_tile(n)
    row = pl.BlockSpec((tm, d), lambda i: (i, 0))
    return pl.pallas_call(
        _ffn_body,
        out_shape=jax.ShapeDtypeStruct((n, d), F32),
        grid=(n // tm,),
        in_specs=[row, _const_spec(npre.shape), _const_spec(wg.shape), _const_spec(wu.shape),
                  _const_spec(wo.shape), _const_spec(npost.shape)],
        out_specs=row,
        compiler_params=_params(("parallel",)),
        name="ffn1",
    )(x, npre, wg, wu, wo, npost)


def _mix_ffn_call(h, oa, ob, wma, wmb, nmix, npre, wg, wu, wo, npost):
    n, d = h.shape
    tm = _token_tile(n)
    row = pl.BlockSpec((tm, d), lambda i: (i, 0))
    half = pl.BlockSpec((tm, oa.shape[1]), lambda i: (i, 0))
    consts = [wma, wmb, nmix, npre, wg, wu, wo, npost]
    return pl.pallas_call(
        _mix_ffn_body,
        out_shape=jax.ShapeDtypeStruct((n, d), F32),
        grid=(n // tm,),
        in_specs=[row, half, half] + [_const_spec(c.shape) for c in consts],
        out_specs=row,
        compiler_params=_params(("parallel",)),
        name="mix_ffn2",
    )(h, oa, ob, *consts)


def _proj_body(widths, h_ref, nrm_ref, w_ref, wba_ref, wbat_ref, *out_refs):
    a = _rms(h_ref[...], nrm_ref[...]).astype(BF16)
    c0 = 0
    for ref, width in zip(out_refs[:len(widths)], widths):
        ref[...] = _dot(a, w_ref[:, c0:c0 + width])
        c0 += width
    ba_ref, bat_ref = out_refs[len(widths):]
    ba_ref[...] = _dot(a, wba_ref[...])
    bat_ref[...] = _dot_nt(wbat_ref[...], a)


def _proj_call(h, nrm, w_main, w_ba, w_bat, widths):
    n, d = h.shape
    tm = _token_tile(n)
    outs = [jax.ShapeDtypeStruct((n, w), F32) for w in widths]
    outs += [jax.ShapeDtypeStruct((n, LANES), F32), jax.ShapeDtypeStruct((SUBLANES, n), F32)]
    out_specs = [pl.BlockSpec((tm, w), lambda i: (i, 0)) for w in widths]
    out_specs += [pl.BlockSpec((tm, LANES), lambda i: (i, 0)), pl.BlockSpec((SUBLANES, tm), lambda i: (0, i))]
    return pl.pallas_call(
        functools.partial(_proj_body, tuple(widths)),
        out_shape=outs,
        grid=(n // tm,),
        in_specs=[pl.BlockSpec((tm, d), lambda i: (i, 0)), _const_spec(nrm.shape), _const_spec(w_main.shape),
                  _const_spec(w_ba.shape), _const_spec(w_bat.shape)],
        out_specs=out_specs,
        compiler_params=_params(("parallel",)),
        name="mix_in_proj",
    )(h, nrm, w_main, w_ba, w_bat)


def _top_blocks_t(scores, valid):
    row = lax.broadcasted_iota(jnp.int32, scores.shape, 0)
    s = jnp.where(valid, scores, NEG_INF)
    rank = jnp.zeros(scores.shape, jnp.int32)
    for m in range(scores.shape[0]):
        sm = s[m:m + 1, :]
        ahead = (sm > s) | ((sm == s) & (row > m))
        rank = rank + jnp.where(ahead, 1, 0)
    return valid & (rank < MOBA_TOPK)


def _moba_prompt_body(head_dim, q_ref, k_ref, v_ref, o_ref, kbf_ref, vt_ref, kmean_ref, sel_ref):
    j = pl.program_id(2)
    n_blocks = vt_ref.shape[0]
    n_heads = q_ref.shape[1] // head_dim
    blk = MOBA_BLOCK

    @pl.when(j == 0)
    def _():
        kbf_ref[...] = k_ref[...].astype(BF16)
        for n in range(n_blocks):
            rows = slice(n * blk, (n + 1) * blk)
            vt_ref[n] = v_ref[rows, :].T.astype(BF16)
            kmean_ref[n:n + 1, :] = jnp.mean(k_ref[rows, :], axis=0, keepdims=True)

    q = q_ref[...] * (head_dim ** -0.5)
    lane = lax.broadcasted_iota(jnp.int32, q.shape, 1)
    key_i = lax.broadcasted_iota(jnp.int32, (blk, blk), 0)
    qry_i = lax.broadcasted_iota(jnp.int32, (blk, blk), 1)
    causal = key_i <= qry_i
    cand = lax.broadcasted_iota(jnp.int32, (n_blocks, blk), 0)
    kmean = kmean_ref[...].astype(BF16)
    k_own = kbf_ref[pl.ds(pl.multiple_of(j * blk, blk), blk), :]

    qhs = []
    for hh in range(n_heads):
        in_head = (lane >= hh * head_dim) & (lane < (hh + 1) * head_dim)
        qhs.append(jnp.where(in_head, q, 0.0).astype(BF16))
    gates = [_dot_nt(kmean, qh) for qh in qhs]
    own_lg = [_dot_nt(k_own, qh) for qh in qhs]
    for hh in range(n_heads):
        sel_ref[hh] = jnp.where(_top_blocks_t(gates[hh], cand < j), 1.0, 0.0)

    def softmax_step(lg, m_run, l_run):
        m_new = jnp.maximum(m_run, jnp.max(lg, axis=0, keepdims=True))
        alpha = jnp.exp(m_run - m_new)
        p = jnp.exp(lg - m_new)
        return m_new, alpha, alpha * l_run + jnp.sum(p, axis=0, keepdims=True), p.astype(BF16)

    def values_t(n, hh):
        return vt_ref[n, hh * head_dim:(hh + 1) * head_dim, :]

    carry = []
    m_init, l_init = jnp.full((1, blk), NEG_INF, F32), jnp.zeros((1, blk), F32)
    own = [softmax_step(jnp.where(causal, own_lg[hh], NEG_INF), m_init, l_init) for hh in range(n_heads)]
    for hh in range(n_heads):
        m0, _, l0, p0 = own[hh]
        carry += [m0, l0, _dot(values_t(j, hh), p0)]

    def past_block(n, carry):
        kb = kbf_ref[pl.ds(pl.multiple_of(n * blk, blk), blk), :]
        lgs = [_dot_nt(kb, qh) for qh in qhs]
        steps = []
        for hh in range(n_heads):
            picked = sel_ref[hh, pl.ds(n, 1), :] > 0.5
            steps.append(softmax_step(jnp.where(picked, lgs[hh], NEG_INF), carry[3 * hh], carry[3 * hh + 1]))
        out = []
        for hh in range(n_heads):
            m_new, alpha, l_new, p16 = steps[hh]
            out += [m_new, l_new, alpha * carry[3 * hh + 2] + _dot(values_t(n, hh), p16)]
        return tuple(out)

    carry = lax.fori_loop(0, j, past_block, tuple(carry))
    out_t = jnp.concatenate([carry[3 * hh + 2] / carry[3 * hh + 1] for hh in range(n_heads)], axis=0)
    o_ref[...] = out_t.T


def _moba_prompt_call(q, k, v, batch, seq, head_dim):
    n, width = q.shape
    lanes = min(width, MOBA_HEAD_LANES)
    groups = width // lanes
    nb = seq // MOBA_BLOCK
    assert nb <= SUBLANES and width % lanes == 0 and lanes % head_dim == 0
    qspec = pl.BlockSpec((MOBA_BLOCK, lanes), lambda b, hp, j: (b * nb + j, hp))
    kvspec = pl.BlockSpec((seq, lanes), lambda b, hp, j: (b, hp))
    return pl.pallas_call(
        functools.partial(_moba_prompt_body, head_dim),
        out_shape=jax.ShapeDtypeStruct((n, width), F32),
        grid=(batch, groups, nb),
        in_specs=[qspec, kvspec, kvspec],
        out_specs=qspec,
        scratch_shapes=[pltpu.VMEM((seq, lanes), BF16),
                        pltpu.VMEM((nb, lanes, MOBA_BLOCK), BF16),
                        pltpu.VMEM((nb, lanes), F32),
                        pltpu.VMEM((lanes // head_dim, nb, MOBA_BLOCK), F32)],
        compiler_params=_params(("parallel", "parallel", "arbitrary")),
        name="moba_prompt",
    )(q, k, v)


def _moba_sample_body(layer, n_new, n_heads, pt_ref, q_ref, kn_ref, vn_ref, k_hbm, v_hbm, o_ref,
                      kbuf, vbuf, sem, ot_ref):
    s = pl.program_id(0)
    n_seq = pl.num_programs(0)
    _, cp, width, page = kbuf.shape
    hd = width // n_heads
    n_chunks = pt_ref.shape[1] // cp
    pages_per_block = MOBA_BLOCK // page
    blocks_per_chunk = cp // pages_per_block
    n_blocks = n_chunks * blocks_per_chunk
    rows = LANES

    def page_copies(seq, chunk, slot, i):
        page_id = pt_ref[seq, chunk * cp + i]
        return (pltpu.make_async_copy(k_hbm.at[layer, page_id], kbuf.at[slot, i], sem.at[slot]),
                pltpu.make_async_copy(v_hbm.at[layer, page_id], vbuf.at[slot, i], sem.at[slot]))

    def start_chunk(seq, chunk, slot):
        for i in range(cp):
            for thread, copy in enumerate(page_copies(seq, chunk, slot, i)):
                copy.start(priority=thread)

    def wait_chunk(slot):
        for i in range(cp):
            for copy in page_copies(0, 0, slot, i):
                copy.wait()

    @pl.when(s == 0)
    def _():
        for c in range(RING_SLOTS - 1):
            start_chunk(0, c, c)

    q = q_ref[0] * (hd ** -0.5)
    lane_w = lax.broadcasted_iota(jnp.int32, (n_heads, width), 1)
    head_w = lax.broadcasted_iota(jnp.int32, (n_heads, width), 0)
    in_head = (lane_w >= head_w * hd) & (lane_w < (head_w + 1) * hd)
    q_rows = [jnp.where(in_head, jnp.broadcast_to(q[t:t + 1, :], (n_heads, width)), 0.0) for t in range(n_new)]
    q_rows.append(jnp.zeros((rows - n_new * n_heads, width), F32))
    qm = jnp.concatenate(q_rows, axis=0).astype(BF16)
    col_head = lax.broadcasted_iota(jnp.int32, (hd, rows), 1) & (n_heads - 1)
    lane = lax.broadcasted_iota(jnp.int32, (rows, LANES), 1)

    def block_stats(kt, vt, mask):
        logits = _dot(qm, kt)
        if mask is not None:
            logits = jnp.where(mask, logits, NEG_INF)
        g = jnp.sum(logits, axis=-1, keepdims=True)
        m = jnp.max(logits, axis=-1, keepdims=True)
        p = jnp.exp(logits - m)
        l = jnp.sum(p, axis=-1, keepdims=True)
        ot_all = _dot_nt(vt, p.astype(BF16))
        ot = jnp.zeros((hd, rows), F32)
        for h in range(n_heads):
            ot = jnp.where(col_head == h, ot_all[h * hd:(h + 1) * hd, :], ot)
        return g, m, l, ot

    def put(stat, b, column):
        return jnp.where(lane == b, column, stat)

    ahead = RING_SLOTS - 1

    def chunk_step(c, stats):
        g = s * n_chunks + c
        slot = lax.rem(g, RING_SLOTS)
        wait_chunk(slot)
        fill = lax.rem(g + ahead, RING_SLOTS)

        @pl.when(c + ahead < n_chunks)
        def _():
            start_chunk(s, c + ahead, fill)

        @pl.when((c + ahead >= n_chunks) & (s + 1 < n_seq))
        def _():
            start_chunk(s + 1, c + ahead - n_chunks, fill)

        g_all, m_all, l_all = stats
        for j in range(blocks_per_chunk):
            pages = range(j * pages_per_block, (j + 1) * pages_per_block)
            kt = jnp.concatenate([kbuf[slot, i] for i in pages], axis=1).astype(BF16)
            vt = jnp.concatenate([vbuf[slot, i] for i in pages], axis=1).astype(BF16)
            g, m, l, ot = block_stats(kt, vt, None)
            b = c * blocks_per_chunk + j
            ot_ref[b] = ot
            g_all, m_all, l_all = put(g_all, b, g), put(m_all, b, m), put(l_all, b, l)
        return g_all, m_all, l_all

    init = (jnp.zeros((rows, LANES), F32),) * 3
    g_all, m_all, l_all = lax.fori_loop(0, n_chunks, chunk_step, init)

    pad_t = lambda a: jnp.concatenate([a, jnp.zeros((LANES - a.shape[0], width), F32)], axis=0).T.astype(BF16)
    row = lax.broadcasted_iota(jnp.int32, (rows, LANES), 0)
    own_ok = (lane * n_heads <= row) & (lane < n_new)
    _, m_own, l_own, ot_own = block_stats(pad_t(kn_ref[0]), pad_t(vn_ref[0]), own_ok)
    m_all, l_all = put(m_all, n_blocks, m_own), put(l_all, n_blocks, l_own)

    gt, mt, lt = g_all.T[:n_blocks], m_all.T, l_all.T
    sel = _top_blocks_t(gt, lax.broadcasted_iota(jnp.int32, gt.shape, 0) < n_blocks)
    m_own_t, l_own_t = mt[n_blocks:n_blocks + 1], lt[n_blocks:n_blocks + 1]
    mt, lt = mt[:n_blocks], lt[:n_blocks]
    m_fin = jnp.maximum(jnp.max(jnp.where(sel, mt, NEG_INF), axis=0, keepdims=True), m_own_t)
    w = jnp.where(sel, jnp.exp(mt - m_fin), 0.0)
    w_own = jnp.exp(m_own_t - m_fin)
    denom = jnp.sum(w * lt, axis=0, keepdims=True) + w_own * l_own_t
    out_t = w_own * ot_own
    for b in range(n_blocks):
        out_t = out_t + w[b:b + 1, :] * ot_ref[b]
    out = (out_t / denom).T
    o_ref[0] = out[:n_new * n_heads].reshape(n_new, n_heads, hd)


def _moba_sample_call(q, kn, vn, pool_kt, pool_vt, layer, page_table, n_heads):
    seqs, n_new, width = q.shape
    page = pool_kt.shape[3]
    hd = width // n_heads
    n_pages = page_table.shape[1]
    cp = PAGES_PER_CHUNK
    n_chunks = n_pages // cp
    n_blocks = n_pages * page // MOBA_BLOCK
    assert n_pages % cp == 0 and n_chunks >= RING_SLOTS - 1
    assert MOBA_BLOCK % page == 0 and cp % (MOBA_BLOCK // page) == 0 and page == LANES
    assert n_blocks < LANES and n_new * n_heads <= LANES and n_new <= SAMPLE_SLOT

    per_seq = lambda a: pl.BlockSpec((1,) + a.shape[1:], lambda s, pt: (s, 0, 0))
    grid_spec = pltpu.PrefetchScalarGridSpec(
        num_scalar_prefetch=1,
        grid=(seqs,),
        in_specs=[per_seq(q), per_seq(kn), per_seq(vn),
                  pl.BlockSpec(memory_space=pl.ANY), pl.BlockSpec(memory_space=pl.ANY)],
        out_specs=pl.BlockSpec((1, n_new, n_heads, hd), lambda s, pt: (s, 0, 0, 0)),
        scratch_shapes=[pltpu.VMEM((RING_SLOTS, cp, width, page), F32),
                        pltpu.VMEM((RING_SLOTS, cp, width, page), F32),
                        pltpu.SemaphoreType.DMA((RING_SLOTS,)),
                        pltpu.VMEM((n_blocks, hd, LANES), F32)],
    )
    return pl.pallas_call(
        functools.partial(_moba_sample_body, layer, n_new, n_heads),
        out_shape=jax.ShapeDtypeStruct((seqs, n_new, n_heads, hd), F32),
        grid_spec=grid_spec,
        compiler_params=_params(("arbitrary",)),
        name="moba_sample",
    )(page_table, q, kn, vn, pool_kt, pool_vt)


def _unit_lower_inverses(l_stricts, chunk):
    n = l_stricts[0].shape[0]
    eye = jnp.where(lax.broadcasted_iota(jnp.int32, (n, n), 0) == lax.broadcasted_iota(jnp.int32, (n, n), 1),
                    1.0, 0.0)
    ms = [-l for l in l_stricts]
    ss = [eye + m for m in ms]
    ps = [_dot(m.astype(BF16), m.astype(BF16)) for m in ms]
    steps = chunk.bit_length() - 2
    for it in range(steps):
        p16s = [p.astype(BF16) for p in ps]
        ss = [s + _dot(p16, s.astype(BF16)) for s, p16 in zip(ss, p16s)]
        if it + 1 < steps:
            ps = [_dot(p16, p16) for p16 in p16s]
    return ss


def _gdn_prepare(xs_ref, base, rows, convw_ref, ba_ref, bat_ref, prow_ref, pcol_ref, chunk, n_heads, hd, valid):
    w = convw_ref[...]
    taps = w.shape[0]
    y = xs_ref[pl.ds(base, rows), :] * w[0:1, :]
    for t in range(1, taps):
        y = y + xs_ref[pl.ds(base + t, rows), :] * w[t:t + 1, :]
    y = _silu(y)
    bw = n_heads * hd
    ba = ba_ref[...]
    bat = bat_ref[...]
    beta_all = jax.nn.sigmoid(ba)
    g_all = -jnp.exp(prow_ref[0:1, :]) * _softplus(ba + prow_ref[1:2, :])
    g_rows = -jnp.exp(pcol_ref[:, 0:1]) * _softplus(bat + pcol_ref[:, 1:2])
    if valid is not None:
        vcol, vrow = valid
        beta_all = jnp.where(vcol, beta_all, 0.0)
        g_all = jnp.where(vcol, g_all, 0.0)
        g_rows = jnp.where(vrow, g_rows, 0.0)
    ri = lax.broadcasted_iota(jnp.int32, (rows, rows), 0)
    ci = lax.broadcasted_iota(jnp.int32, (rows, rows), 1)
    shift = chunk.bit_length() - 1
    same = (ri >> shift) == (ci >> shift)
    lower = jnp.where(same & (ci <= ri), 1.0, 0.0)
    upper = jnp.where(same & (ri <= ci), 1.0, 0.0)
    hi = lax.Precision.HIGHEST
    gc_cols = _dot(lower, g_all, hi)
    gc_rows = _dot(g_rows, upper, hi)
    heads = []
    for h in range(n_heads):
        q = y[:, h * hd:(h + 1) * hd]
        k = y[:, bw + h * hd:bw + (h + 1) * hd]
        v = y[:, 2 * bw + h * hd:2 * bw + (h + 1) * hd]
        q = q * lax.rsqrt(jnp.sum(q * q, axis=-1, keepdims=True) + L2_EPS) * (hd ** -0.5)
        k = k * lax.rsqrt(jnp.sum(k * k, axis=-1, keepdims=True) + L2_EPS)
        if valid is not None:
            k = jnp.where(valid[0], k, 0.0)
            v = jnp.where(valid[0], v, 0.0)
        beta = beta_all[:, h:h + 1]
        gcc = gc_cols[:, n_heads + h:n_heads + h + 1]
        gcr = gc_rows[n_heads + h:n_heads + h + 1, :]
        heads.append((q, k, v, beta, gcc, gcr))
    return heads, same, ri, ci


def _gdn_matrices(heads, same, ri, ci, chunk):
    causal = same & (ci <= ri)
    strict = same & (ci < ri)
    hd = heads[0][0].shape[1]
    decays, kbs, k16s, lmats, attns = [], [], [], [], []
    for q, k, v, beta, gcc, gcr in heads:
        decays.append(jnp.where(causal, jnp.exp(jnp.where(causal, gcc - gcr, 0.0)), 0.0))
        kbs.append(k * beta)
        k16s.append(k.astype(BF16))
    for (q, k, v, beta, gcc, gcr), decay, kb, k16 in zip(heads, decays, kbs, k16s):
        lmats.append(jnp.where(strict, _dot_nt(kb.astype(BF16), k16) * decay, 0.0))
        attns.append((_dot_nt(q.astype(BF16), k16) * decay).astype(BF16))
    tinvs = _unit_lower_inverses(lmats, chunk)
    out = []
    for (q, k, v, beta, gcc, gcr), kb, tinv, attn16 in zip(heads, kbs, tinvs, attns):
        egc = jnp.exp(gcc)
        rhs = jnp.concatenate([v * beta, kb * egc], axis=1).astype(BF16)
        uw = _dot(tinv.astype(BF16), rhs)
        out.append((uw[:, :hd], uw[:, hd:], attn16, q * egc))
    return out


def _gated_out(o, gnorm_row, z):
    return _rms(o, gnorm_row) * _silu(z)


def _gdn_prompt_body(n_heads, hd, qkv_ref, z_ref, ba_ref, bat_ref, convw_ref, prow_ref, pcol_ref, gnorm_ref,
                     o_ref, sfin_ref, xs_ref, s_ref, vn_ref):
    i = pl.program_id(1)
    rows = qkv_ref.shape[0]
    chunk = GDN_CHUNK

    @pl.when(i == 0)
    def _():
        xs_ref[0:SUBLANES, :] = jnp.zeros((SUBLANES, xs_ref.shape[1]), F32)
        s_ref[...] = jnp.zeros(s_ref.shape, F32)

    x = qkv_ref[...]
    xs_ref[SUBLANES:SUBLANES + rows, :] = x
    taps = convw_ref.shape[0]
    heads, same, ri, ci = _gdn_prepare(xs_ref, SUBLANES - taps + 1, rows, convw_ref, ba_ref, bat_ref, prow_ref,
                                       pcol_ref, chunk, n_heads, hd, None)
    xs_ref[0:SUBLANES, :] = x[rows - SUBLANES:rows, :]

    gnorm = gnorm_ref[...]
    mats = _gdn_matrices(heads, same, ri, ci, chunk)
    vn_ref[...] = jnp.zeros(vn_ref.shape, BF16)
    states = [s_ref[h] for h in range(n_heads)]
    for c in range(rows // chunk):
        r0 = c * chunk
        sl = slice(r0, r0 + chunk)
        s16s = [s.astype(BF16) for s in states]
        v_news = [u[sl] - _dot(w[sl].astype(BF16), s16) for (u, w, _, _), s16 in zip(mats, s16s)]
        for h, v_new in enumerate(v_news):
            vn_ref[h, sl, :] = v_new.astype(BF16)
        outs = [_dot(qg[sl].astype(BF16), s16) + _dot(attn16[sl, :], vn_ref[h])
                for h, ((_, _, attn16, qg), s16) in enumerate(zip(mats, s16s))]
        for h, ((q, k, v, beta, gcc, gcr), v_new) in enumerate(zip(heads, v_news)):
            gl = gcc[r0 + chunk - 1:r0 + chunk, :]
            kd = (k[sl] * jnp.exp(gl - gcc[sl])).astype(BF16)
            states[h] = states[h] * jnp.exp(gl) + _dot_tn(kd, v_new.astype(BF16))
        for h, o in enumerate(outs):
            o_ref[sl, h * hd:(h + 1) * hd] = _gated_out(o, gnorm, z_ref[sl, h * hd:(h + 1) * hd])
    for h in range(n_heads):
        s_ref[h] = states[h]

    @pl.when(i == pl.num_programs(1) - 1)
    def _():
        sfin_ref[0] = s_ref[...]


def _gdn_prompt_call(qkv, z, ba, bat, convw, prow, pcol, gnorm, batch, seq, n_heads, hd):
    n, cdim = qkv.shape
    rows = GDN_ROWS
    nblk = seq // rows
    bw = n_heads * hd
    tile = lambda width: pl.BlockSpec((rows, width), lambda b, i: (b * nblk + i, 0))
    return pl.pallas_call(
        functools.partial(_gdn_prompt_body, n_heads, hd),
        out_shape=[jax.ShapeDtypeStruct((n, bw), F32), jax.ShapeDtypeStruct((batch, n_heads, hd, hd), F32)],
        grid=(batch, nblk),
        in_specs=[tile(cdim), tile(bw), tile(LANES),
                  pl.BlockSpec((SUBLANES, rows), lambda b, i: (0, b * nblk + i)),
                  _const_spec(convw.shape), _const_spec(prow.shape), _const_spec(pcol.shape),
                  _const_spec(gnorm.shape)],
        out_specs=[tile(bw), pl.BlockSpec((1, n_heads, hd, hd), lambda b, i: (b, 0, 0, 0))],
        scratch_shapes=[pltpu.VMEM((SUBLANES + rows, cdim), F32), pltpu.VMEM((n_heads, hd, hd), F32),
                        pltpu.VMEM((n_heads, rows, hd), BF16)],
        compiler_params=_params(("parallel", "arbitrary")),
        name="gdn_prompt",
    )(qkv, z, ba, bat, convw, prow, pcol, gnorm)


def _gdn_sample_body(n_heads, hd, n_new, xp_ref, z_ref, ba_ref, bat_ref, s0_ref, convw_ref, prow_ref, pcol_ref,
                     gnorm_ref, o_ref, snew_ref, xs_ref):
    rows = xp_ref.shape[0]
    slot = SAMPLE_SLOT
    xs_ref[0:rows, :] = xp_ref[...]
    xs_ref[rows:rows + SUBLANES, :] = jnp.zeros((SUBLANES, xs_ref.shape[1]), F32)
    vcol = (lax.broadcasted_iota(jnp.int32, (rows, 1), 0) & (slot - 1)) < n_new
    vrow = (lax.broadcasted_iota(jnp.int32, (1, rows), 1) & (slot - 1)) < n_new
    heads, same, ri, ci = _gdn_prepare(xs_ref, 0, rows, convw_ref, ba_ref, bat_ref, prow_ref, pcol_ref,
                                       slot, n_heads, hd, (vcol, vrow))
    gnorm = gnorm_ref[...]
    mats = _gdn_matrices(heads, same, ri, ci, slot)
    for h, ((q, k, v, beta, gcc, gcr), (u, w, attn16, qg)) in enumerate(zip(heads, mats)):
        v_parts, qs_parts = [], []
        for g in range(rows // slot):
            r0 = g * slot
            s16 = s0_ref[g, h].astype(BF16)
            lhs = jnp.concatenate([w[r0:r0 + slot], qg[r0:r0 + slot]], axis=0).astype(BF16)
            both = _dot(lhs, s16)
            v_parts.append(u[r0:r0 + slot] - both[:slot])
            qs_parts.append(both[slot:])
        v_new = jnp.concatenate(v_parts, axis=0)
        o = jnp.concatenate(qs_parts, axis=0) + _dot(attn16, v_new.astype(BF16))
        o_ref[:, h * hd:(h + 1) * hd] = _gated_out(o, gnorm, z_ref[:, h * hd:(h + 1) * hd])
        for g in range(rows // slot):
            r0 = g * slot
            gl = gcc[r0 + slot - 1:r0 + slot, :]
            kd = k[r0:r0 + slot] * jnp.exp(gl - gcc[r0:r0 + slot])
            snew_ref[g, h] = s0_ref[g, h] * jnp.exp(gl) + _dot_tn(kd, v_new[r0:r0 + slot])


def _gdn_sample_call(xp, z, ba, bat, s0, convw, prow, pcol, gnorm, n_heads, hd, n_new):
    n, cdim = xp.shape
    rows = SAMPLE_GROUP * SAMPLE_SLOT
    bw = n_heads * hd
    tile = lambda width: pl.BlockSpec((rows, width), lambda i: (i, 0))
    sspec = pl.BlockSpec((SAMPLE_GROUP, n_heads, hd, hd), lambda i: (i, 0, 0, 0))
    return pl.pallas_call(
        functools.partial(_gdn_sample_body, n_heads, hd, n_new),
        out_shape=[jax.ShapeDtypeStruct((n, bw), F32), jax.ShapeDtypeStruct(s0.shape, F32)],
        grid=(n // rows,),
        in_specs=[tile(cdim), tile(bw), tile(LANES), pl.BlockSpec((SUBLANES, rows), lambda i: (0, i)), sspec,
                  _const_spec(convw.shape), _const_spec(prow.shape), _const_spec(pcol.shape),
                  _const_spec(gnorm.shape)],
        out_specs=[tile(bw), sspec],
        scratch_shapes=[pltpu.VMEM((rows + SUBLANES, cdim), F32)],
        compiler_params=_params(("parallel",)),
        name="gdn_sample",
    )(xp, z, ba, bat, s0, convw, prow, pcol, gnorm)


def _row(v):
    return v.reshape(1, -1).astype(F32)


def kernel(x_prompt, x_sample, cache_k, cache_v, state_gdn, state_conv, page_table, norm_ffn1_pre, norm_ffn1_post, w_ffn1_in, w_ffn1_out, norm_mix_pre, norm_mix_post, w_mix_in, w_mix_out, gdn_conv_w, gdn_a_log, gdn_dt_bias, gdn_norm_w, norm_ffn2_pre, norm_ffn2_post, w_ffn2_in, w_ffn2_out):
    batch, seq, d_model = x_prompt.shape
    dec_batch, dec_seq, _ = x_sample.shape
    depth = w_ffn1_in.shape[0]
    d_ff = w_ffn1_out.shape[1]
    a_heads, a_hd = cache_k.shape[3], cache_k.shape[4]
    b_heads, b_hd = state_gdn.shape[2], state_gdn.shape[3]
    aw, bw = a_heads * a_hd, b_heads * b_hd
    conv_dim = gdn_conv_w.shape[2]
    taps = gdn_conv_w.shape[1]
    widths = (aw, aw, aw, conv_dim, bw)
    main = sum(widths)
    assert w_mix_in.shape[2] == main + 2 * b_heads and 2 * b_heads <= SUBLANES
    assert seq % GDN_ROWS == 0 and seq % MOBA_BLOCK == 0 and dec_batch % SAMPLE_GROUP == 0
    assert taps - 1 + dec_seq <= SAMPLE_SLOT and dec_seq >= taps - 1 and seq >= taps - 1
    pad = SAMPLE_SLOT - (taps - 1) - dec_seq

    yp = x_prompt.reshape(batch * seq, d_model)
    ys = x_sample.reshape(dec_batch * dec_seq, d_model)
    outs = [[] for _ in range(8)]
    for l in range(depth):
        wg1, wu1 = w_ffn1_in[l, :, :d_ff].astype(BF16), w_ffn1_in[l, :, d_ff:].astype(BF16)
        wo1 = w_ffn1_out[l].astype(BF16)
        wg2, wu2 = w_ffn2_in[l, :, :d_ff].astype(BF16), w_ffn2_in[l, :, d_ff:].astype(BF16)
        wo2 = w_ffn2_out[l].astype(BF16)
        w_main = w_mix_in[l, :, :main].astype(BF16)
        w_tail = w_mix_in[l, :, main:]
        w_ba = jnp.pad(w_tail, ((0, 0), (0, LANES - 2 * b_heads))).astype(BF16)
        w_bat = jnp.pad(w_tail.T, ((0, SUBLANES - 2 * b_heads), (0, 0))).astype(BF16)
        wma, wmb = w_mix_out[l, :aw].astype(BF16), w_mix_out[l, aw:].astype(BF16)
        a_log, dt_bias = gdn_a_log[l].astype(F32), gdn_dt_bias[l].astype(F32)
        prow = jnp.zeros((SUBLANES, LANES), F32)
        prow = prow.at[0, b_heads:2 * b_heads].set(a_log).at[1, b_heads:2 * b_heads].set(dt_bias)
        pcol = jnp.zeros((SUBLANES, LANES), F32)
        pcol = pcol.at[b_heads:2 * b_heads, 0].set(a_log).at[b_heads:2 * b_heads, 1].set(dt_bias)
        convw = gdn_conv_w[l].astype(F32)
        gnorm = _row(gdn_norm_w[l])

        def trunk_in(x):
            h = _ffn_call(x, _row(norm_ffn1_pre[l]), wg1, wu1, wo1, _row(norm_ffn1_post[l]))
            return (h,) + tuple(_proj_call(h, _row(norm_mix_pre[l]), w_main, w_ba, w_bat, widths))

        def trunk_out(h, oa, ob):
            return _mix_ffn_call(h, oa, ob, wma, wmb, _row(norm_mix_post[l]), _row(norm_ffn2_pre[l]),
                                 wg2, wu2, wo2, _row(norm_ffn2_post[l]))

        hp, qa, ka, va, qkv, z, ba, bat = trunk_in(yp)
        oa = _moba_prompt_call(qa, ka, va, batch, seq, a_hd)
        ob, s_p = _gdn_prompt_call(qkv, z, ba, bat, convw, prow, pcol, gnorm, batch, seq, b_heads, b_hd)
        yp = trunk_out(hp, oa, ob)
        outs[0].append(ka.reshape(batch, seq, a_heads, a_hd))
        outs[1].append(va.reshape(batch, seq, a_heads, a_hd))
        outs[4].append(s_p)
        outs[6].append(qkv.reshape(batch, seq, conv_dim)[:, seq - (taps - 1):, :])

        hs, qa, ka, va, qkv, z, ba, bat = trunk_in(ys)
        seq3 = lambda a: a.reshape(dec_batch, dec_seq, a.shape[-1])
        slots = lambda a: jnp.pad(seq3(a), ((0, 0), (0, SAMPLE_SLOT - dec_seq), (0, 0)))
        pages_t = lambda c: jnp.transpose(c, (0, 1, 3, 4, 2)).reshape(c.shape[0], c.shape[1], aw, c.shape[2])
        oa = _moba_sample_call(seq3(qa), slots(ka), slots(va), pages_t(cache_k), pages_t(cache_v), l,
                               page_table, a_heads)
        qkv3 = seq3(qkv)
        slot_rows = lambda a: slots(a).reshape(dec_batch * SAMPLE_SLOT, a.shape[-1])
        xp = jnp.concatenate([state_conv[l].astype(F32), qkv3, jnp.zeros((dec_batch, pad, conv_dim), F32)],
                             axis=1).reshape(dec_batch * SAMPLE_SLOT, conv_dim)
        bat_slots = jnp.pad(bat.reshape(SUBLANES, dec_batch, dec_seq),
                            ((0, 0), (0, 0), (0, SAMPLE_SLOT - dec_seq))).reshape(SUBLANES, dec_batch * SAMPLE_SLOT)
        ob_slots, s_s = _gdn_sample_call(xp, slot_rows(z), slot_rows(ba), bat_slots, state_gdn[l].astype(F32),
                                         convw, prow, pcol, gnorm, b_heads, b_hd, dec_seq)
        ob = ob_slots.reshape(dec_batch, SAMPLE_SLOT, bw)[:, :dec_seq].reshape(dec_batch * dec_seq, bw)
        ys = trunk_out(hs, oa.reshape(dec_batch * dec_seq, aw), ob)
        outs[2].append(ka.reshape(dec_batch, dec_seq, a_heads, a_hd))
        outs[3].append(va.reshape(dec_batch, dec_seq, a_heads, a_hd))
        outs[5].append(s_s)
        outs[7].append(jnp.concatenate([state_conv[l].astype(F32), qkv3], axis=1)[:, dec_seq:, :])

    stacked = [jnp.stack(o) for o in outs]
    return (yp.reshape(batch, seq, d_model), ys.reshape(dec_batch, dec_seq, d_model), *stacked)
```
